```python
import math
import jax
import jax.numpy as jnp
from jax import lax
import numpy as np

D_MODEL = 1024
BATCH = 2
SEQ = 8192
DEPTH = 1

GRID_W = 64
CTX_LEN = 256
N_HEADS = 8
HEAD_DIM = 64
V_DIM = 2 * HEAD_DIM
QK_W = N_HEADS * 2 * HEAD_DIM
ATT_W = N_HEADS * V_DIM
CONV_W = D_MODEL
CONV_K = 3
D_FF = 2816
N_MOD = 9
Q_BLOCK = 128
ROPE_BASE = 10000.0
AXIS_DIM = HEAD_DIM // 2
N_FREQ = AXIS_DIM // 2
EPS = 1e-6
Q_OFF = 0
K_OFF = Q_OFF + QK_W
V_OFF = K_OFF + QK_W
B_OFF = V_OFF + ATT_W
C_OFF = B_OFF + CONV_W
X_OFF = C_OFF + CONV_W
GA_OFF = X_OFF + CONV_W
GB_OFF = GA_OFF + D_MODEL
IN_W = GB_OFF + D_MODEL

kernel_name = "hybrid_diffattn_shortconv_macaron_dit_layer"


def rms_norm(x, g):
    xf = x.astype(jnp.float32)
    y = xf * lax.rsqrt(jnp.mean(xf * xf, axis=-1, keepdims=True) + EPS)
    return (y * g.astype(jnp.float32)).astype(x.dtype)


def modulate(x, shift, scale):
    return x * (1.0 + scale) + shift


def adaln(cond, w_mod, b_mod):
    m = jax.nn.silu(cond) @ w_mod + b_mod
    return jnp.split(m, N_MOD, axis=-1)


def swiglu(x, w_gate, w_up, w_down):
    return (jax.nn.silu(x @ w_gate) * (x @ w_up)) @ w_down


def ffn_sublayer(h, mods, pre_g, post_g, w_gate, w_up, w_down):
    shift, scale, gate = mods
    y = swiglu(modulate(rms_norm(h, pre_g), shift, scale), w_gate, w_up, w_down)
    return h + 0.5 * gate * rms_norm(y, post_g)


def axial_rope_tables(n, dtype):
    n_rows = n // GRID_W
    row = jnp.repeat(jnp.arange(n_rows, dtype=jnp.float32), GRID_W)
    col = jnp.tile(jnp.arange(GRID_W, dtype=jnp.float32), n_rows)
    inv_freq = ROPE_BASE ** (-2.0 * jnp.arange(N_FREQ, dtype=jnp.float32) / AXIS_DIM)
    ang = jnp.stack([row, col], axis=-1)[:, :, None] * inv_freq
    ang = jnp.stack([ang, ang], axis=-2).reshape(n, HEAD_DIM)
    return jnp.cos(ang).astype(dtype), jnp.sin(ang).astype(dtype)


def apply_rope(x, cos, sin):
    xs = x.reshape(x.shape[:-1] + (2, 2, N_FREQ))
    rot = jnp.stack([-xs[..., 1, :], xs[..., 0, :]], axis=-2).reshape(x.shape)
    return x * cos[None, :, None, None, :] + rot * sin[None, :, None, None, :]


def diff_attend(q, k, v, lam):
    s = jnp.einsum('bqhmd,bkhmd->bhmqk', q, k, preferred_element_type=jnp.float32) * (HEAD_DIM ** -0.5)
    p = jax.nn.softmax(s, axis=-1)
    a = p[:, :, 0] - lam * p[:, :, 1]
    return jnp.einsum('bhqk,bkhe->bqhe', a.astype(v.dtype), v)


def blocked_diff_attention(q, k, v, lam):
    b, n = q.shape[:2]
    nb = n // Q_BLOCK
    qb = q.reshape((b, nb, Q_BLOCK) + q.shape[2:]).swapaxes(0, 1)
    ob = lax.map(lambda qi: diff_attend(qi, k, v, lam), qb)
    return ob.swapaxes(0, 1).reshape(b, n, N_HEADS, V_DIM)


def short_conv(z, w):
    L = z.shape[1]
    pad = CONV_K // 2
    zp = jnp.pad(z, ((0, 0), (pad, pad), (0, 0)))
    return sum(zp[:, j:j + L] * w[j] for j in range(CONV_K))


def split_kv(u_kv):
    b, L = u_kv.shape[:2]
    k = u_kv[..., :QK_W].reshape(b, L, N_HEADS, 2, HEAD_DIM)
    v = u_kv[..., QK_W:].reshape(b, L, N_HEADS, V_DIM)
    return k, v


def diff_lambda(lq1, lk1, lq2, lk2, lam_init):
    f = jnp.float32
    return (jnp.exp(jnp.sum(lq1.astype(f) * lk1.astype(f)))
            - jnp.exp(jnp.sum(lq2.astype(f) * lk2.astype(f))) + lam_init)


def token_mixer(u, ext_k, ext_v, rope, lam, lam_init, subln_g, conv_w, w_pa, w_pb, w_o):
    b, L, _ = u.shape
    q = u[..., Q_OFF:K_OFF].reshape(b, L, N_HEADS, 2, HEAD_DIM)
    k, v = split_kv(u[..., K_OFF:B_OFF])
    if rope is not None:
        cos, sin = rope
        q = apply_rope(q, cos, sin)
        k = apply_rope(k, cos, sin)
    if ext_k is not None:
        k = jnp.concatenate([ext_k, k], axis=1)
        v = jnp.concatenate([ext_v, v], axis=1)
    o = blocked_diff_attention(q, k, v, lam)
    o = rms_norm(o, subln_g) * (1.0 - lam_init)
    y_att = o.reshape(b, L, ATT_W) @ w_pa
    gate_b = u[..., B_OFF:C_OFF]
    gate_c = u[..., C_OFF:X_OFF]
    x_in = u[..., X_OFF:GA_OFF]
    y_conv = (gate_b * short_conv(gate_c * x_in, conv_w)) @ w_pb
    merged = (jax.nn.sigmoid(u[..., GA_OFF:GB_OFF]) * y_att
              + jax.nn.sigmoid(u[..., GB_OFF:IN_W]) * y_conv)
    return merged @ w_o


def setup_inputs(seed: int = 0) -> dict:
    key = jax.random.key(seed)
    ks = jax.random.split(key, 32)
    L = DEPTH

    def nrm(k, shape, scale):
        return scale * jax.random.normal(k, shape, jnp.float32)

    def gain(k, shape):
        return 1.0 + 0.02 * jax.random.normal(k, shape, jnp.float32)

    d_s = D_MODEL ** -0.5
    f_s = D_FF ** -0.5
    return {
        "x": nrm(ks[0], (BATCH, SEQ, D_MODEL), 1.0),
        "c": nrm(ks[1], (BATCH, D_MODEL), 1.0),
        "ctx": nrm(ks[2], (BATCH, CTX_LEN, D_MODEL), 1.0),
        "c_ctx": nrm(ks[3], (D_MODEL,), 1.0),
        "w_mod": nrm(ks[4], (L, D_MODEL, N_MOD * D_MODEL), 0.5 * d_s),
        "b_mod": nrm(ks[5], (L, N_MOD * D_MODEL), 0.02),
        "ffn1_pre_g": gain(ks[6], (L, D_MODEL)),
        "ffn1_post_g": gain(ks[7], (L, D_MODEL)),
        "ffn1_w_gate": nrm(ks[8], (L, D_MODEL, D_FF), d_s),
        "ffn1_w_up": nrm(ks[9], (L, D_MODEL, D_FF), d_s),
        "ffn1_w_down": nrm(ks[10], (L, D_FF, D_MODEL), f_s),
        "mix_pre_g": gain(ks[11], (L, D_MODEL)),
        "mix_post_g": gain(ks[12], (L, D_MODEL)),
        "w_in": nrm(ks[13], (L, D_MODEL, IN_W), d_s),
        "lam_q1": nrm(ks[14], (L, HEAD_DIM), 0.1),
        "lam_k1": nrm(ks[15], (L, HEAD_DIM), 0.1),
        "lam_q2": nrm(ks[16], (L, HEAD_DIM), 0.1),
        "lam_k2": nrm(ks[17], (L, HEAD_DIM), 0.1),
        "attn_subln_g": gain(ks[18], (L, V_DIM)),
        "conv_w": nrm(ks[19], (L, CONV_K, CONV_W), CONV_K ** -0.5),
        "w_attn_proj": nrm(ks[20], (L, ATT_W, D_MODEL), ATT_W ** -0.5),
        "w_conv_proj": nrm(ks[21], (L, CONV_W, D_MODEL), CONV_W ** -0.5),
        "w_out": nrm(ks[22], (L, D_MODEL, D_MODEL), d_s),
        "ffn2_pre_g": gain(ks[23], (L, D_MODEL)),
        "ffn2_post_g": gain(ks[24], (L, D_MODEL)),
        "ffn2_w_gate": nrm(ks[25], (L, D_MODEL, D_FF), d_s),
        "ffn2_w_up": nrm(ks[26], (L, D_MODEL, D_FF), d_s),
        "ffn2_w_down": nrm(ks[27], (L, D_FF, D_MODEL), f_s),
    }


def reference(x, c, ctx, c_ctx, w_mod, b_mod,
              ffn1_pre_g, ffn1_post_g, ffn1_w_gate, ffn1_w_up, ffn1_w_down,
              mix_pre_g, mix_post_g, w_in, lam_q1, lam_k1, lam_q2, lam_k2,
              attn_subln_g, conv_w, w_attn_proj, w_conv_proj, w_out,
              ffn2_pre_g, ffn2_post_g, ffn2_w_gate, ffn2_w_up, ffn2_w_down):
    n = x.shape[1]
    rope = axial_rope_tables(n, x.dtype)
    h, hc = x, ctx
    for l in range(DEPTH):
        last = l == DEPTH - 1
        ml = [t[:, None, :] for t in adaln(c, w_mod[l], b_mod[l])]
        mc = adaln(c_ctx, w_mod[l], b_mod[l])

        ffn1 = (ffn1_pre_g[l], ffn1_post_g[l], ffn1_w_gate[l], ffn1_w_up[l], ffn1_w_down[l])
        h = ffn_sublayer(h, ml[0:3], *ffn1)
        hc = ffn_sublayer(hc, mc[0:3], *ffn1)

        lam_init = 0.8 - 0.6 * math.exp(-0.3 * l)
        lam = diff_lambda(lam_q1[l], lam_k1[l], lam_q2[l], lam_k2[l], lam_init)
        mix = (lam, lam_init, attn_subln_g[l], conv_w[l], w_attn_proj[l], w_conv_proj[l], w_out[l])
        xmc = modulate(rms_norm(hc, mix_pre_g[l]), mc[3], mc[4])
        if last:
            u_kv_c = xmc @ w_in[l][:, K_OFF:B_OFF]
        else:
            uc = xmc @ w_in[l]
            u_kv_c = uc[..., K_OFF:B_OFF]
            yc = token_mixer(uc, None, None, None, *mix)
            hc_mixed = hc + mc[5] * rms_norm(yc, mix_post_g[l])
        k_c, v_c = split_kv(u_kv_c)
        xm = modulate(rms_norm(h, mix_pre_g[l]), ml[3], ml[4])
        y = token_mixer(xm @ w_in[l], k_c, v_c, rope, *mix)
        h = h + ml[5] * rms_norm(y, mix_post_g[l])

        ffn2 = (ffn2_pre_g[l], ffn2_post_g[l], ffn2_w_gate[l], ffn2_w_up[l], ffn2_w_down[l])
        h = ffn_sublayer(h, ml[6:9], *ffn2)
        if not last:
            hc = ffn_sublayer(hc_mixed, mc[6:9], *ffn2)
    return h
```

```python
import functools
import math

import jax
import jax.numpy as jnp
from jax import lax
from jax.experimental import pallas as pl
from jax.experimental.pallas import tpu as pltpu

F32 = jnp.float32
BF16 = jnp.bfloat16

GRID_W = 64
N_HEADS = 8
HEAD_DIM = 64
V_DIM = 2 * HEAD_DIM
CONV_K = 3
N_MOD = 9
ROPE_BASE = 10000.0
AXIS_DIM = HEAD_DIM // 2
N_FREQ = AXIS_DIM // 2
EPS = 1e-6
LAM_INIT = 0.8 - 0.6 * math.exp(-0.3 * 0)
Q_SCALE = (HEAD_DIM ** -0.5) * math.log2(math.e)

LANES = 128
MOD_ROWS = 8
VMEM_LIMIT = 56 * 1024 * 1024

ROW_BLOCK = 512
FF_CHUNK = 256
Q_BLOCK = 256
K_BLOCK = 512


def _dot(a, b):
    return jnp.dot(a, b, preferred_element_type=F32)


def _const_spec(shape):
    return pl.BlockSpec(shape, lambda *_: (0,) * len(shape), pipeline_mode=pl.Buffered(1))


def _rms(x, g):
    return x * lax.rsqrt(jnp.mean(x * x, axis=-1, keepdims=True) + EPS) * g


def _params(n_grid):
    return pltpu.CompilerParams(
        dimension_semantics=("arbitrary",) * n_grid, vmem_limit_bytes=VMEM_LIMIT)


def _adaln_kernel(cond_ref, w_ref, b_ref, o_ref):
    x = cond_ref[...]
    a = x * jax.nn.sigmoid(x)
    a_hi = a.astype(BF16)
    a_lo = (a - a_hi.astype(F32)).astype(BF16)
    w = w_ref[...]
    w_hi = w.astype(BF16)
    w_lo = (w - w_hi.astype(F32)).astype(BF16)
    o_ref[...] = _dot(a_hi, w_hi) + _dot(a_lo, w_hi) + _dot(a_hi, w_lo) + b_ref[...]


def _adaln(cond, w_mod, b_mod):
    d, n = w_mod.shape
    bn = n // 8
    return pl.pallas_call(
        _adaln_kernel,
        grid=(n // bn,),
        in_specs=[
            _const_spec((MOD_ROWS, d)),
            pl.BlockSpec((d, bn), lambda j: (0, j)),
            pl.BlockSpec((1, bn), lambda j: (0, j)),
        ],
        out_specs=pl.BlockSpec((MOD_ROWS, bn), lambda j: (0, j)),
        out_shape=jax.ShapeDtypeStruct((MOD_ROWS, n), F32),
        compiler_params=_params(1),
        name="adaln",
    )(cond, w_mod, b_mod.reshape(1, n))


def _ffn_kernel(x_ref, mod_ref, pre_ref, post_ref, wg_ref, wu_ref, wd_ref, o_ref, a_ref, *, mod_base):
    x = x_ref[...]
    shift = mod_ref[0, mod_base:mod_base + 1, :]
    scale = mod_ref[0, mod_base + 1:mod_base + 2, :]
    gate = mod_ref[0, mod_base + 2:mod_base + 3, :]
    xm = (_rms(x, pre_ref[...]) * (1.0 + scale) + shift).astype(BF16)
    d_ff = wg_ref.shape[1]
    for c in range(d_ff // FF_CHUNK):
        sl = slice(c * FF_CHUNK, (c + 1) * FF_CHUNK)
        g = _dot(xm, wg_ref[:, sl])
        u = _dot(xm, wu_ref[:, sl])
        a_ref[:, sl] = (g * jax.nn.sigmoid(g) * u).astype(BF16)
    y = _dot(a_ref[...], wd_ref[...])
    o_ref[...] = x + 0.5 * gate * _rms(y, post_ref[...])


def _ffn(x, mods, pre_g, post_g, wg, wu, wd, *, mod_base, mod_row):
    rows, d = x.shape
    d_ff = wg.shape[1]
    tm = min(ROW_BLOCK, rows)
    return pl.pallas_call(
        functools.partial(_ffn_kernel, mod_base=mod_base),
        grid=(rows // tm,),
        in_specs=[
            pl.BlockSpec((tm, d), lambda i: (i, 0)),
            pl.BlockSpec((1, N_MOD, d), lambda i: (mod_row(i), 0, 0)),
            _const_spec((1, d)),
            _const_spec((1, d)),
            _const_spec((d, d_ff)),
            _const_spec((d, d_ff)),
            _const_spec((d_ff, d)),
        ],
        out_specs=pl.BlockSpec((tm, d), lambda i: (i, 0)),
        out_shape=jax.ShapeDtypeStruct((rows, d), F32),
        scratch_shapes=[pltpu.VMEM((tm, d_ff), BF16)],
        compiler_params=_params(1),
        name="ffn",
    )(x, mods, pre_g, post_g, wg, wu, wd)


def _ctx_kv_kernel(x_ref, mod_ref, g_ref, wk_ref, wv_ref, k_ref, v_ref):
    x = x_ref[...]
    shift = mod_ref[0, 3:4, :]
    scale = mod_ref[0, 4:5, :]
    xm = (_rms(x, g_ref[...]) * (1.0 + scale) + shift).astype(BF16)
    k = _dot(xm, wk_ref[...])
    v = _dot(xm, wv_ref[...])
    for h in range(N_HEADS):
        sl = slice(h * LANES, (h + 1) * LANES)
        k_ref[0, h] = k[:, sl].astype(BF16)
        v_ref[0, h] = v[:, sl].astype(BF16)


def _ctx_kv(hc, mods, g, w_in, *, batch, ctx_len, mod_row):
    d = hc.shape[1]
    head_spec = pl.BlockSpec((1, N_HEADS, ctx_len, LANES), lambda b: (b, 0, 0, 0))
    out = jax.ShapeDtypeStruct((batch, N_HEADS, ctx_len, LANES), BF16)
    return pl.pallas_call(
        _ctx_kv_kernel,
        grid=(batch,),
        in_specs=[
            pl.BlockSpec((ctx_len, d), lambda b: (b, 0)),
            pl.BlockSpec((1, N_MOD, d), lambda b: (mod_row, 0, 0)),
            _const_spec((1, d)),
            pl.BlockSpec((d, d), lambda b: (0, 1), pipeline_mode=pl.Buffered(1)),
            pl.BlockSpec((d, d), lambda b: (0, 2), pipeline_mode=pl.Buffered(1)),
        ],
        out_specs=[head_spec, head_spec],
        out_shape=[out, out],
        compiler_params=_params(1),
        name="ctx_kv",
    )(hc, mods, g, w_in, w_in)


def _rope_tables(n):
    t = jnp.arange(n, dtype=jnp.int32)
    row = (t // GRID_W).astype(F32)
    col = (t % GRID_W).astype(F32)
    inv_freq = ROPE_BASE ** (-2.0 * jnp.arange(N_FREQ, dtype=F32) / AXIS_DIM)
    ang_r = row[:, None] * inv_freq
    ang_c = col[:, None] * inv_freq
    ang = jnp.concatenate([ang_r, ang_r, ang_c, ang_c] * 2, axis=-1)
    cos, sin = jnp.cos(ang), jnp.sin(ang)
    first_half = (jnp.arange(LANES) % AXIS_DIM) < N_FREQ
    sin_up = jnp.where(first_half, -sin, 0.0)
    sin_dn = jnp.where(first_half, 0.0, sin)
    return cos, sin_up, sin_dn


def _proj_kernel(x_ref, mod_ref, g_ref, w_ref, cos_ref, sup_ref, sdn_ref,
                 q_ref, k_ref, v_ref, z_ref, gb_ref, sga_ref, sgb_ref):
    d = x_ref.shape[1]
    x = x_ref[...]
    shift = mod_ref[0, 3:4, :]
    scale = mod_ref[0, 4:5, :]
    xm = (_rms(x, g_ref[...]) * (1.0 + scale) + shift).astype(BF16)
    cos, sup, sdn = cos_ref[...], sup_ref[...], sdn_ref[...]

    def group(j):
        return _dot(xm, w_ref[:, j * d:(j + 1) * d])

    def rope(xh):
        return (xh * cos + pltpu.roll(xh, LANES - N_FREQ, axis=1) * sup
                + pltpu.roll(xh, N_FREQ, axis=1) * sdn)

    uq = group(0)
    for h in range(N_HEADS):
        q_ref[0, h] = (rope(uq[:, h * LANES:(h + 1) * LANES]) * Q_SCALE).astype(BF16)
    uk = group(1)
    for h in range(N_HEADS):
        k_ref[0, h] = rope(uk[:, h * LANES:(h + 1) * LANES]).astype(BF16)
    uv = group(2)
    for h in range(N_HEADS):
        v_ref[0, h] = uv[:, h * LANES:(h + 1) * LANES].astype(BF16)
    gb_ref[...] = group(3).astype(BF16)
    z_ref[...] = (group(4) * group(5)).astype(BF16)
    sga_ref[...] = jax.nn.sigmoid(group(6)).astype(BF16)
    sgb_ref[...] = jax.nn.sigmoid(group(7)).astype(BF16)


def _proj(h, mods, g, w_in, tables, *, batch, seq):
    rows, d = h.shape
    tm = ROW_BLOCK
    bpb = seq // tm
    head_spec = pl.BlockSpec((1, N_HEADS, tm, LANES), lambda i: (i // bpb, 0, i % bpb, 0))
    head_out = jax.ShapeDtypeStruct((batch, N_HEADS, seq, LANES), BF16)
    row_spec = pl.BlockSpec((tm, d), lambda i: (i, 0))
    row_out = jax.ShapeDtypeStruct((rows, d), BF16)
    tab_spec = pl.BlockSpec((tm, LANES), lambda i: (i % bpb, 0))
    return pl.pallas_call(
        _proj_kernel,
        grid=(rows // tm,),
        in_specs=[
            row_spec,
            pl.BlockSpec((1, N_MOD, d), lambda i: (i // bpb, 0, 0)),
            _const_spec((1, d)),
            _const_spec(w_in.shape),
            tab_spec, tab_spec, tab_spec,
        ],
        out_specs=[head_spec, head_spec, head_spec, row_spec, row_spec, row_spec, row_spec],
        out_shape=[head_out, head_out, head_out, row_out, row_out, row_out, row_out],
        compiler_params=_params(1),
        name="proj",
    )(h, mods, g, w_in, *tables)


def _attn_kernel(lam_ref, g_ref, q_ref, k_ref, v_ref, kc_ref, vc_ref, o_ref,
                 qq_ref, m_ref, l_ref, acc_ref, *, n_kblocks):
    tq = q_ref.shape[2]
    q = q_ref[0, 0]
    lane = lax.broadcasted_iota(jnp.int32, q.shape, 1)
    zero = jnp.zeros_like(q)
    qq_ref[0:tq, :] = jnp.where(lane < HEAD_DIM, q, zero)
    qq_ref[tq:, :] = jnp.where(lane >= HEAD_DIM, q, zero)
    m_ref[...] = jnp.full(m_ref.shape, -jnp.inf, F32)
    l_ref[...] = jnp.zeros(l_ref.shape, F32)
    acc_ref[...] = jnp.zeros(acc_ref.shape, F32)

    def step(kj, vj):
        s = lax.dot_general(qq_ref[...], kj, (((1,), (1,)), ((), ())),
                            preferred_element_type=F32)
        m_old = m_ref[...]
        m_new = jnp.maximum(m_old, jnp.max(s, axis=-1, keepdims=True))
        alpha = jnp.exp2(m_old - m_new)
        p = jnp.exp2(s - m_new)
        l_ref[...] = alpha * l_ref[...] + jnp.sum(p, axis=-1, keepdims=True)
        acc_ref[...] = alpha * acc_ref[...] + _dot(p.astype(BF16), vj)
        m_ref[...] = m_new

    def body(j, carry):
        off = pl.multiple_of(j * K_BLOCK, K_BLOCK)
        step(k_ref[0, 0, pl.ds(off, K_BLOCK), :], v_ref[0, 0, pl.ds(off, K_BLOCK), :])
        return carry

    lax.fori_loop(0, n_kblocks, body, 0)
    step(kc_ref[0, 0], vc_ref[0, 0])

    lv = lam_ref[...]
    lam = (jnp.exp(jnp.sum(lv[0:1] * lv[1:2], axis=-1, keepdims=True))
           - jnp.exp(jnp.sum(lv[2:3] * lv[3:4], axis=-1, keepdims=True)) + LAM_INIT)
    l = l_ref[...]
    acc = acc_ref[...]
    o = acc[0:tq] / l[0:tq] - lam * (acc[tq:] / l[tq:])
    o_ref[0, 0] = (_rms(o, g_ref[...]) * (1.0 - LAM_INIT)).astype(BF16)


def _attn(lam_vecs, subln_g, q, k, v, kc, vc):
    batch, heads, seq, _ = q.shape
    ctx_len = kc.shape[2]
    tq = Q_BLOCK
    kv_spec = pl.BlockSpec((1, 1, seq, LANES), lambda b, h, i: (b, h, 0, 0))
    ctx_spec = pl.BlockSpec((1, 1, ctx_len, LANES), lambda b, h, i: (b, h, 0, 0))
    q_spec = pl.BlockSpec((1, 1, tq, LANES), lambda b, h, i: (b, h, i, 0))
    return pl.pallas_call(
        functools.partial(_attn_kernel, n_kblocks=seq // K_BLOCK),
        grid=(batch, heads, seq // tq),
        in_specs=[
            _const_spec(lam_vecs.shape),
            _const_spec(subln_g.shape),
            q_spec, kv_spec, kv_spec, ctx_spec, ctx_spec,
        ],
        out_specs=q_spec,
        out_shape=jax.ShapeDtypeStruct(q.shape, BF16),
        scratch_shapes=[
            pltpu.VMEM((2 * tq, LANES), BF16),
            pltpu.VMEM((2 * tq, 1), F32),
            pltpu.VMEM((2 * tq, 1), F32),
            pltpu.VMEM((2 * tq, LANES), F32),
        ],
        compiler_params=_params(3),
        name="attn",
    )(lam_vecs, subln_g, q, k, v, kc, vc)


def _mixout_kernel(h_ref, mod_ref, o_ref, z_ref, zp_ref, zn_ref, gb_ref, sga_ref, sgb_ref,
                   cw_ref, wpa_ref, wpb_ref, wo_ref, post_ref, out_ref, *, blocks_per_seq):
    tm = h_ref.shape[0]
    i = pl.program_id(0)
    o = jnp.concatenate([o_ref[0, h] for h in range(N_HEADS)], axis=1)
    y_att = _dot(o, wpa_ref[...])

    z = z_ref[...].astype(F32)
    pos = i % blocks_per_seq
    zp = zp_ref[...].astype(F32)
    zn = zn_ref[...].astype(F32)
    prev_row = jnp.where(pos == 0, 0.0, zp[zp.shape[0] - 1:, :])
    next_row = jnp.where(pos == blocks_per_seq - 1, 0.0, zn[0:1, :])
    rows = lax.broadcasted_iota(jnp.int32, z.shape, 0)
    z_prev = jnp.where(rows == 0, prev_row, pltpu.roll(z, 1, axis=0))
    z_next = jnp.where(rows == tm - 1, next_row, pltpu.roll(z, tm - 1, axis=0))
    conv = z_prev * cw_ref[0:1, :] + z * cw_ref[1:2, :] + z_next * cw_ref[2:3, :]
    y_conv = _dot((gb_ref[...].astype(F32) * conv).astype(BF16), wpb_ref[...])

    merged = sga_ref[...].astype(F32) * y_att + sgb_ref[...].astype(F32) * y_conv
    y = _dot(merged.astype(BF16), wo_ref[...])
    out_ref[...] = h_ref[...] + mod_ref[0, 5:6, :] * _rms(y, post_ref[...])


def _mixout(h, mods, o, z, gb, sga, sgb, conv_w, wpa, wpb, wo, post_g, *, seq):
    rows, d = h.shape
    tm = ROW_BLOCK
    bpb = seq // tm
    halo = 16
    hpb = tm // halo
    n_halo = rows // halo
    row_spec = pl.BlockSpec((tm, d), lambda i: (i, 0))
    return pl.pallas_call(
        functools.partial(_mixout_kernel, blocks_per_seq=bpb),
        grid=(rows // tm,),
        in_specs=[
            row_spec,
            pl.BlockSpec((1, N_MOD, d), lambda i: (i // bpb, 0, 0)),
            pl.BlockSpec((1, N_HEADS, tm, LANES), lambda i: (i // bpb, 0, i % bpb, 0)),
            row_spec,
            pl.BlockSpec((halo, d), lambda i: (jnp.maximum(i * hpb - 1, 0), 0)),
            pl.BlockSpec((halo, d), lambda i: (jnp.minimum((i + 1) * hpb, n_halo - 1), 0)),
            row_spec, row_spec, row_spec,
            _const_spec(conv_w.shape),
            _const_spec(wpa.shape), _const_spec(wpb.shape), _const_spec(wo.shape),
            _const_spec((1, d)),
        ],
        out_specs=row_spec,
        out_shape=jax.ShapeDtypeStruct((rows, d), F32),
        compiler_params=_params(1),
        name="mixout",
    )(h, mods, o, z, z, z, gb, sga, sgb, conv_w, wpa, wpb, wo, post_g)


def kernel(x, c, ctx, c_ctx, w_mod, b_mod, ffn1_pre_g, ffn1_post_g, ffn1_w_gate, ffn1_w_up, ffn1_w_down, mix_pre_g, mix_post_g, w_in, lam_q1, lam_k1, lam_q2, lam_k2, attn_subln_g, conv_w, w_attn_proj, w_conv_proj, w_out, ffn2_pre_g, ffn2_post_g, ffn2_w_gate, ffn2_w_up, ffn2_w_down):
    batch, seq, d = x.shape
    ctx_len = ctx.shape[1]
    assert w_mod.shape[0] == 1, "single-layer stack only"
    assert batch + 1 <= MOD_ROWS and seq % ROW_BLOCK == 0 and seq % GRID_W == 0
    bpb = seq // ROW_BLOCK

    cond = jnp.zeros((MOD_ROWS, d), F32).at[:batch].set(c).at[batch].set(c_ctx)
    mods = _adaln(cond, w_mod[0], b_mod[0]).reshape(MOD_ROWS, N_MOD, d)

    def row(v):
        return v.reshape(1, -1)

    ffn1_w = [w[0].astype(BF16) for w in (ffn1_w_gate, ffn1_w_up, ffn1_w_down)]
    ffn2_w = [w[0].astype(BF16) for w in (ffn2_w_gate, ffn2_w_up, ffn2_w_down)]
    w_in_b = w_in[0].astype(BF16)

    h = _ffn(x.reshape(batch * seq, d), mods, row(ffn1_pre_g), row(ffn1_post_g), *ffn1_w,
             mod_base=0, mod_row=lambda i: i // bpb)
    hc = _ffn(ctx.reshape(batch * ctx_len, d), mods, row(ffn1_pre_g), row(ffn1_post_g), *ffn1_w,
              mod_base=0, mod_row=lambda i: batch)

    kc, vc = _ctx_kv(hc, mods, row(mix_pre_g), w_in_b, batch=batch, ctx_len=ctx_len, mod_row=batch)
    q, k, v, z, gb, sga, sgb = _proj(h, mods, row(mix_pre_g), w_in_b, _rope_tables(seq),
                                     batch=batch, seq=seq)
    lam_vecs = jnp.concatenate([lam_q1, lam_k1, lam_q2, lam_k2], axis=0)
    o = _attn(lam_vecs, attn_subln_g, q, k, v, kc, vc)
    h = _mixout(h, mods, o, z, gb, sga, sgb, conv_w[0], w_attn_proj[0].astype(BF16),
                w_conv_proj[0].astype(BF16), w_out[0].astype(BF16), row(mix_post_g), seq=seq)
    h = _ffn(h, mods, row(ffn2_pre_g), row(ffn2_post_g), *ffn2_w,
             mod_base=6, mod_row=lambda i: i // bpb)
    return h.reshape(batch, seq, d)
```

```python
import functools
import math

import jax
import jax.numpy as jnp
from jax import lax
from jax.experimental import pallas as pl
from jax.experimental.pallas import tpu as pltpu

F32 = jnp.float32
BF16 = jnp.bfloat16

GRID_W = 64
N_HEADS = 8
HEAD_DIM = 64
V_DIM = 2 * HEAD_DIM
CONV_K = 3
N_MOD = 9
ROPE_BASE = 10000.0
AXIS_DIM = HEAD_DIM // 2
N_FREQ = AXIS_DIM // 2
EPS = 1e-6
LAM_INIT = 0.8 - 0.6 * math.exp(-0.3 * 0)
Q_SCALE = (HEAD_DIM ** -0.5) * math.log2(math.e)

LANES = 128
SUBLANES = 8
MOD_ROWS = 8
VMEM_LIMIT = 56 * 1024 * 1024

ROW_BLOCK = 512
FF_CHUNK = 256
Q_BLOCK = 128
K_CHUNK = 256
K_UNROLL = 11


def _dot(a, b):
    return jnp.dot(a, b, preferred_element_type=F32)


def _const_spec(shape):
    return pl.BlockSpec(shape, lambda *_: (0,) * len(shape), pipeline_mode=pl.Buffered(1))


def _rms(x, g):
    return x * lax.rsqrt(jnp.mean(x * x, axis=-1, keepdims=True) + EPS) * g


def _params(n_grid):
    return pltpu.CompilerParams(
        dimension_semantics=("arbitrary",) * n_grid, vmem_limit_bytes=VMEM_LIMIT)


def _adaln_kernel(cond_ref, w_ref, b_ref, o_ref):
    x = cond_ref[...]
    a = x * jax.nn.sigmoid(x)
    a_hi = a.astype(BF16)
    a_lo = (a - a_hi.astype(F32)).astype(BF16)
    w = w_ref[...]
    w_hi = w.astype(BF16)
    w_lo = (w - w_hi.astype(F32)).astype(BF16)
    o_ref[...] = _dot(a_hi, w_hi) + _dot(a_lo, w_hi) + _dot(a_hi, w_lo) + b_ref[...]


def _adaln(cond, w_mod, b_mod):
    d, n = w_mod.shape
    bn = n // 8
    return pl.pallas_call(
        _adaln_kernel,
        grid=(n // bn,),
        in_specs=[
            _const_spec((MOD_ROWS, d)),
            pl.BlockSpec((d, bn), lambda j: (0, j)),
            pl.BlockSpec((1, bn), lambda j: (0, j)),
        ],
        out_specs=pl.BlockSpec((MOD_ROWS, bn), lambda j: (0, j)),
        out_shape=jax.ShapeDtypeStruct((MOD_ROWS, n), F32),
        compiler_params=_params(1),
        name="adaln",
    )(cond, w_mod, b_mod.reshape(1, n))


def _ffn_kernel(x_ref, mod_ref, pre_ref, post_ref, wg_ref, wu_ref, wd_ref, o_ref, a_ref, *, mod_base):
    x = x_ref[...]
    shift = mod_ref[0, mod_base:mod_base + 1, :]
    scale = mod_ref[0, mod_base + 1:mod_base + 2, :]
    gate = mod_ref[0, mod_base + 2:mod_base + 3, :]
    xm = (_rms(x, pre_ref[...]) * (1.0 + scale) + shift).astype(BF16)
    d_ff = wg_ref.shape[1]
    for c in range(d_ff // FF_CHUNK):
        sl = slice(c * FF_CHUNK, (c + 1) * FF_CHUNK)
        g = _dot(xm, wg_ref[:, sl])
        u = _dot(xm, wu_ref[:, sl])
        a_ref[:, sl] = (g * jax.nn.sigmoid(g) * u).astype(BF16)
    y = _dot(a_ref[...], wd_ref[...])
    o_ref[...] = x + 0.5 * gate * _rms(y, post_ref[...])


def _ffn(x, mods, pre_g, post_g, wg, wu, wd, *, mod_base, mod_row):
    rows, d = x.shape
    d_ff = wg.shape[1]
    tm = min(ROW_BLOCK, rows)
    return pl.pallas_call(
        functools.partial(_ffn_kernel, mod_base=mod_base),
        grid=(rows // tm,),
        in_specs=[
            pl.BlockSpec((tm, d), lambda i: (i, 0)),
            pl.BlockSpec((1, N_MOD, d), lambda i: (mod_row(i), 0, 0)),
            _const_spec((1, d)),
            _const_spec((1, d)),
            _const_spec((d, d_ff)),
            _const_spec((d, d_ff)),
            _const_spec((d_ff, d)),
        ],
        out_specs=pl.BlockSpec((tm, d), lambda i: (i, 0)),
        out_shape=jax.ShapeDtypeStruct((rows, d), F32),
        scratch_shapes=[pltpu.VMEM((tm, d_ff), BF16)],
        compiler_params=_params(1),
        name="ffn",
    )(x, mods, pre_g, post_g, wg, wu, wd)


def _ctx_kv_kernel(x_ref, mod_ref, g_ref, wk_ref, wv_ref, k_ref, vt_ref):
    x = x_ref[...]
    shift = mod_ref[0, 3:4, :]
    scale = mod_ref[0, 4:5, :]
    xm = (_rms(x, g_ref[...]) * (1.0 + scale) + shift).astype(BF16)
    k = _dot(xm, wk_ref[...])
    v = _dot(xm, wv_ref[...])
    for h in range(N_HEADS):
        sl = slice(h * LANES, (h + 1) * LANES)
        k_ref[0, h] = k[:, sl].astype(BF16)
        vt_ref[0, h, 0] = v[:, sl].T.astype(BF16)


def _ctx_kv(hc, mods, g, w_in, *, batch, ctx_len, mod_row):
    d = hc.shape[1]
    assert ctx_len == K_CHUNK
    return pl.pallas_call(
        _ctx_kv_kernel,
        grid=(batch,),
        in_specs=[
            pl.BlockSpec((ctx_len, d), lambda b: (b, 0)),
            pl.BlockSpec((1, N_MOD, d), lambda b: (mod_row, 0, 0)),
            _const_spec((1, d)),
            pl.BlockSpec((d, d), lambda b: (0, 1), pipeline_mode=pl.Buffered(1)),
            pl.BlockSpec((d, d), lambda b: (0, 2), pipeline_mode=pl.Buffered(1)),
        ],
        out_specs=[
            pl.BlockSpec((1, N_HEADS, ctx_len, LANES), lambda b: (b, 0, 0, 0)),
            pl.BlockSpec((1, N_HEADS, 1, LANES, ctx_len), lambda b: (b, 0, 0, 0, 0)),
        ],
        out_shape=[
            jax.ShapeDtypeStruct((batch, N_HEADS, ctx_len, LANES), BF16),
            jax.ShapeDtypeStruct((batch, N_HEADS, 1, LANES, ctx_len), BF16),
        ],
        compiler_params=_params(1),
        name="ctx_kv",
    )(hc, mods, g, w_in, w_in)


def _rope_tables(n):
    t = jnp.arange(n, dtype=jnp.int32)
    row = (t // GRID_W).astype(F32)
    col = (t % GRID_W).astype(F32)
    inv_freq = ROPE_BASE ** (-2.0 * jnp.arange(N_FREQ, dtype=F32) / AXIS_DIM)
    ang_r = row[:, None] * inv_freq
    ang_c = col[:, None] * inv_freq
    ang = jnp.concatenate([ang_r, ang_r, ang_c, ang_c] * 2, axis=-1)
    cos, sin = jnp.cos(ang), jnp.sin(ang)
    first_half = (jnp.arange(LANES) % AXIS_DIM) < N_FREQ
    sin_up = jnp.where(first_half, -sin, 0.0)
    sin_dn = jnp.where(first_half, 0.0, sin)
    return cos, sin_up, sin_dn


def _proj_kernel(x_ref, mod_ref, g_ref, w_ref, cos_ref, sup_ref, sdn_ref,
                 qt_ref, k_ref, vt_ref, z_ref, gb_ref, sga_ref, sgb_ref):
    tm, d = x_ref.shape
    x = x_ref[...]
    shift = mod_ref[0, 3:4, :]
    scale = mod_ref[0, 4:5, :]
    xm = (_rms(x, g_ref[...]) * (1.0 + scale) + shift).astype(BF16)
    cos, sup, sdn = cos_ref[...], sup_ref[...], sdn_ref[...]

    def group(j):
        return _dot(xm, w_ref[:, j * d:(j + 1) * d])

    def rope(xh):
        return (xh * cos + pltpu.roll(xh, LANES - N_FREQ, axis=1) * sup
                + pltpu.roll(xh, N_FREQ, axis=1) * sdn)

    uq = group(0)
    for h in range(N_HEADS):
        qh = rope(uq[:, h * LANES:(h + 1) * LANES]) * Q_SCALE
        for r in range(tm // Q_BLOCK):
            qt_ref[0, h, r] = qh[r * Q_BLOCK:(r + 1) * Q_BLOCK, :].T.astype(BF16)
    uk = group(1)
    for h in range(N_HEADS):
        k_ref[0, h] = rope(uk[:, h * LANES:(h + 1) * LANES]).astype(BF16)
    uv = group(2)
    for h in range(N_HEADS):
        vh = uv[:, h * LANES:(h + 1) * LANES]
        for r in range(tm // K_CHUNK):
            vt_ref[0, h, r] = vh[r * K_CHUNK:(r + 1) * K_CHUNK, :].T.astype(BF16)
    gb_ref[...] = group(3).astype(BF16)
    z_ref[...] = (group(4) * group(5)).astype(BF16)
    sga_ref[...] = jax.nn.sigmoid(group(6)).astype(BF16)
    sgb_ref[...] = jax.nn.sigmoid(group(7)).astype(BF16)


def _proj(h, mods, g, w_in, tables, *, batch, seq):
    rows, d = h.shape
    tm = ROW_BLOCK
    bpb = seq // tm
    qpb, cpb = tm // Q_BLOCK, tm // K_CHUNK
    row_spec = pl.BlockSpec((tm, d), lambda i: (i, 0))
    row_out = jax.ShapeDtypeStruct((rows, d), BF16)
    tab_spec = pl.BlockSpec((tm, LANES), lambda i: (i % bpb, 0))
    return pl.pallas_call(
        _proj_kernel,
        grid=(rows // tm,),
        in_specs=[
            row_spec,
            pl.BlockSpec((1, N_MOD, d), lambda i: (i // bpb, 0, 0)),
            _const_spec((1, d)),
            _const_spec(w_in.shape),
            tab_spec, tab_spec, tab_spec,
        ],
        out_specs=[
            pl.BlockSpec((1, N_HEADS, qpb, LANES, Q_BLOCK), lambda i: (i // bpb, 0, i % bpb, 0, 0)),
            pl.BlockSpec((1, N_HEADS, tm, LANES), lambda i: (i // bpb, 0, i % bpb, 0)),
            pl.BlockSpec((1, N_HEADS, cpb, LANES, K_CHUNK), lambda i: (i // bpb, 0, i % bpb, 0, 0)),
            row_spec, row_spec, row_spec, row_spec,
        ],
        out_shape=[
            jax.ShapeDtypeStruct((batch, N_HEADS, seq // Q_BLOCK, LANES, Q_BLOCK), BF16),
            jax.ShapeDtypeStruct((batch, N_HEADS, seq, LANES), BF16),
            jax.ShapeDtypeStruct((batch, N_HEADS, seq // K_CHUNK, LANES, K_CHUNK), BF16),
            row_out, row_out, row_out, row_out,
        ],
        compiler_params=_params(1),
        name="proj",
    )(h, mods, g, w_in, *tables)


def _attn_kernel(lam_ref, g_ref, qt_ref, k_ref, vt_ref, o_ref, s0_ref, s1_ref, *, n_qblocks, n_chunks):
    tq = Q_BLOCK
    two_tq = 2 * tq
    s_refs = (s0_ref, s1_ref)

    lv = lam_ref[...]
    lam = (jnp.exp(jnp.sum(lv[0:1] * lv[1:2], axis=-1, keepdims=True))
           - jnp.exp(jnp.sum(lv[2:3] * lv[3:4], axis=-1, keepdims=True)) + LAM_INIT)

    def load_qq(i):
        qt = qt_ref[0, 0, i]
        feat = lax.broadcasted_iota(jnp.int32, qt.shape, 0)
        zero = jnp.zeros_like(qt)
        return jnp.concatenate(
            [jnp.where(feat < HEAD_DIM, qt, zero), jnp.where(feat >= HEAD_DIM, qt, zero)], axis=1)

    def fold(x, op):
        return op(x.reshape(K_CHUNK // SUBLANES, SUBLANES, two_tq), axis=0)

    def stage_a(j, s_ref, qq, mx):
        off = pl.multiple_of(j * K_CHUNK, K_CHUNK)
        s = _dot(k_ref[0, 0, pl.ds(off, K_CHUNK), :], qq)
        s_ref[j] = s
        return jnp.maximum(mx, fold(s, jnp.max))

    def stage_b(j, s_ref, m, ls, acc):
        e = jnp.exp2(s_ref[j] - m)
        ls = ls + fold(e, jnp.sum)
        acc = acc + _dot(vt_ref[0, 0, j], e.astype(BF16))
        return ls, acc

    mx0 = jnp.full((SUBLANES, two_tq), -jnp.inf, F32)
    ls0 = jnp.zeros((SUBLANES, two_tq), F32)
    acc0 = jnp.zeros((LANES, two_tq), F32)

    def finalize(i, ls, acc):
        l = jnp.sum(ls, axis=0, keepdims=True)
        o = acc[:, :tq] * (1.0 / l[:, :tq]) - acc[:, tq:] * (lam / l[:, tq:])
        on = o * lax.rsqrt(jnp.mean(o * o, axis=0, keepdims=True) + EPS) * g_ref[...] * (1.0 - LAM_INIT)
        off = pl.multiple_of(i * tq, tq)
        o_ref[0, 0, pl.ds(off, tq), :] = on.T.astype(BF16)

    def run_a(i, parity):
        qq = load_qq(i)
        mx = lax.fori_loop(0, n_chunks, lambda j, mx: stage_a(j, s_refs[parity], qq, mx), mx0,
                           unroll=K_UNROLL)
        return jnp.max(mx, axis=0, keepdims=True)

    def run_b(i, parity, m):
        ls, acc = lax.fori_loop(0, n_chunks, lambda j, c: stage_b(j, s_refs[parity], m, *c),
                                (ls0, acc0), unroll=K_UNROLL)
        finalize(i, ls, acc)

    def run_ab(i, parity, m_prev):
        qq = load_qq(i)

        def body(j, c):
            mx, ls, acc = c
            mx = stage_a(j, s_refs[parity], qq, mx)
            ls, acc = stage_b(j, s_refs[1 - parity], m_prev, ls, acc)
            return mx, ls, acc

        mx, ls, acc = lax.fori_loop(0, n_chunks, body, (mx0, ls0, acc0), unroll=K_UNROLL)
        finalize(i - 1, ls, acc)
        return jnp.max(mx, axis=0, keepdims=True)

    def pair(p, m):
        m = run_ab(2 * p + 1, 1, m)
        return run_ab(2 * p + 2, 0, m)

    assert n_qblocks % 2 == 0
    m = run_a(0, 0)
    m = lax.fori_loop(0, n_qblocks // 2 - 1, pair, m)
    m = run_ab(n_qblocks - 1, 1, m)
    run_b(n_qblocks - 1, 1, m)


def _attn(lam_vecs, subln_g, qt, k_all, vt_all):
    batch, heads, n_qblocks = qt.shape[:3]
    n_keys = k_all.shape[2]
    n_chunks = n_keys // K_CHUNK
    assert n_chunks % K_UNROLL == 0
    seq = n_qblocks * Q_BLOCK
    return pl.pallas_call(
        functools.partial(_attn_kernel, n_qblocks=n_qblocks, n_chunks=n_chunks),
        grid=(batch, heads),
        in_specs=[
            _const_spec(lam_vecs.shape),
            _const_spec(subln_g.shape),
            pl.BlockSpec((1, 1, n_qblocks, LANES, Q_BLOCK), lambda b, h: (b, h, 0, 0, 0)),
            pl.BlockSpec((1, 1, n_keys, LANES), lambda b, h: (b, h, 0, 0)),
            pl.BlockSpec((1, 1, n_chunks, LANES, K_CHUNK), lambda b, h: (b, h, 0, 0, 0)),
        ],
        out_specs=pl.BlockSpec((1, 1, seq, LANES), lambda b, h: (b, h, 0, 0)),
        out_shape=jax.ShapeDtypeStruct((batch, heads, seq, LANES), BF16),
        scratch_shapes=[pltpu.VMEM((n_chunks, K_CHUNK, 2 * Q_BLOCK), F32)] * 2,
        compiler_params=_params(2),
        name="attn",
    )(lam_vecs, subln_g, qt, k_all, vt_all)


def _mixout_kernel(h_ref, mod_ref, o_ref, z_ref, zp_ref, zn_ref, gb_ref, sga_ref, sgb_ref,
                   cw_ref, wpa_ref, wpb_ref, wo_ref, post_ref, out_ref, *, blocks_per_seq):
    tm = h_ref.shape[0]
    i = pl.program_id(0)
    o = jnp.concatenate([o_ref[0, h] for h in range(N_HEADS)], axis=1)
    y_att = _dot(o, wpa_ref[...])

    z = z_ref[...].astype(F32)
    pos = i % blocks_per_seq
    zp = zp_ref[...].astype(F32)
    zn = zn_ref[...].astype(F32)
    prev_row = jnp.where(pos == 0, 0.0, zp[zp.shape[0] - 1:, :])
    next_row = jnp.where(pos == blocks_per_seq - 1, 0.0, zn[0:1, :])
    rows = lax.broadcasted_iota(jnp.int32, z.shape, 0)
    z_prev = jnp.where(rows == 0, prev_row, pltpu.roll(z, 1, axis=0))
    z_next = jnp.where(rows == tm - 1, next_row, pltpu.roll(z, tm - 1, axis=0))
    conv = z_prev * cw_ref[0:1, :] + z * cw_ref[1:2, :] + z_next * cw_ref[2:3, :]
    y_conv = _dot((gb_ref[...].astype(F32) * conv).astype(BF16), wpb_ref[...])

    merged = sga_ref[...].astype(F32) * y_att + sgb_ref[...].astype(F32) * y_conv
    y = _dot(merged.astype(BF16), wo_ref[...])
    out_ref[...] = h_ref[...] + mod_ref[0, 5:6, :] * _rms(y, post_ref[...])


def _mixout(h, mods, o, z, gb, sga, sgb, conv_w, wpa, wpb, wo, post_g, *, seq):
    rows, d = h.shape
    tm = ROW_BLOCK
    bpb = seq // tm
    halo = 16
    hpb = tm // halo
    n_halo = rows // halo
    row_spec = pl.BlockSpec((tm, d), lambda i: (i, 0))
    return pl.pallas_call(
        functools.partial(_mixout_kernel, blocks_per_seq=bpb),
        grid=(rows // tm,),
        in_specs=[
            row_spec,
            pl.BlockSpec((1, N_MOD, d), lambda i: (i // bpb, 0, 0)),
            pl.BlockSpec((1, N_HEADS, tm, LANES), lambda i: (i // bpb, 0, i % bpb, 0)),
            row_spec,
            pl.BlockSpec((halo, d), lambda i: (jnp.maximum(i * hpb - 1, 0), 0)),
            pl.BlockSpec((halo, d), lambda i: (jnp.minimum((i + 1) * hpb, n_halo - 1), 0)),
            row_spec, row_spec, row_spec,
            _const_spec(conv_w.shape),
            _const_spec(wpa.shape), _const_spec(wpb.shape), _const_spec(wo.shape),
            _const_spec((1, d)),
        ],
        out_specs=row_spec,
        out_shape=jax.ShapeDtypeStruct((rows, d), F32),
        compiler_params=_params(1),
        name="mixout",
    )(h, mods, o, z, z, z, gb, sga, sgb, conv_w, wpa, wpb, wo, post_g)


def kernel(x, c, ctx, c_ctx, w_mod, b_mod, ffn1_pre_g, ffn1_post_g, ffn1_w_gate, ffn1_w_up, ffn1_w_down, mix_pre_g, mix_post_g, w_in, lam_q1, lam_k1, lam_q2, lam_k2, attn_subln_g, conv_w, w_attn_proj, w_conv_proj, w_out, ffn2_pre_g, ffn2_post_g, ffn2_w_gate, ffn2_w_up, ffn2_w_down):
    batch, seq, d = x.shape
    ctx_len = ctx.shape[1]
    assert w_mod.shape[0] == 1, "single-layer stack only"
    assert batch + 1 <= MOD_ROWS and seq % ROW_BLOCK == 0 and seq % GRID_W == 0
    bpb = seq // ROW_BLOCK

    cond = jnp.zeros((MOD_ROWS, d), F32).at[:batch].set(c).at[batch].set(c_ctx)
    mods = _adaln(cond, w_mod[0], b_mod[0]).reshape(MOD_ROWS, N_MOD, d)

    def row(v):
        return v.reshape(1, -1)

    ffn1_w = [w[0].astype(BF16) for w in (ffn1_w_gate, ffn1_w_up, ffn1_w_down)]
    ffn2_w = [w[0].astype(BF16) for w in (ffn2_w_gate, ffn2_w_up, ffn2_w_down)]
    w_in_b = w_in[0].astype(BF16)

    h = _ffn(x.reshape(batch * seq, d), mods, row(ffn1_pre_g), row(ffn1_post_g), *ffn1_w,
             mod_base=0, mod_row=lambda i: i // bpb)
    hc = _ffn(ctx.reshape(batch * ctx_len, d), mods, row(ffn1_pre_g), row(ffn1_post_g), *ffn1_w,
              mod_base=0, mod_row=lambda i: batch)

    kc, vct = _ctx_kv(hc, mods, row(mix_pre_g), w_in_b, batch=batch, ctx_len=ctx_len, mod_row=batch)
    qt, k, vt, z, gb, sga, sgb = _proj(h, mods, row(mix_pre_g), w_in_b, _rope_tables(seq),
                                       batch=batch, seq=seq)
    k_all = jnp.concatenate([k, kc], axis=2)
    vt_all = jnp.concatenate([vt, vct], axis=2)
    lam_vecs = jnp.concatenate([lam_q1, lam_k1, lam_q2, lam_k2], axis=0)
    o = _attn(lam_vecs, attn_subln_g.reshape(V_DIM, 1), qt, k_all, vt_all)
    h = _mixout(h, mods, o, z, gb, sga, sgb, conv_w[0], w_attn_proj[0].astype(BF16),
                w_conv_proj[0].astype(BF16), w_out[0].astype(BF16), row(mix_post_g), seq=seq)
    h = _ffn(h, mods, row(ffn2_pre_g), row(ffn2_post_g), *ffn2_w,
             mod_base=6, mod_row=lambda i: i // bpb)
    return h.reshape(batch, seq, d)
```

```python
import functools
import math

import jax
import jax.numpy as jnp
from jax import lax
from jax.experimental import pallas as pl
from jax.experimental.pallas import tpu as pltpu

F32 = jnp.float32
BF16 = jnp.bfloat16

GRID_W = 64
N_HEADS = 8
HEAD_DIM = 64
V_DIM = 2 * HEAD_DIM
CONV_K = 3
N_MOD = 9
ROPE_BASE = 10000.0
AXIS_DIM = HEAD_DIM // 2
N_FREQ = AXIS_DIM // 2
EPS = 1e-6
LAM_INIT = 0.8 - 0.6 * math.exp(-0.3 * 0)
Q_SCALE = (HEAD_DIM ** -0.5) * math.log2(math.e)

LANES = 128
SUBLANES = 8
MOD_ROWS = 8
VMEM_LIMIT = 56 * 1024 * 1024

ROW_BLOCK = 512
FF_CHUNK = 256
Q_BLOCK = 128
K_CHUNK = 256


def _dot(a, b):
    return jnp.dot(a, b, preferred_element_type=F32)


def _const_spec(shape):
    return pl.BlockSpec(shape, lambda *_: (0,) * len(shape), pipeline_mode=pl.Buffered(1))


def _rms(x, g):
    return x * lax.rsqrt(jnp.mean(x * x, axis=-1, keepdims=True) + EPS) * g


def _params(n_grid):
    return pltpu.CompilerParams(
        dimension_semantics=("arbitrary",) * n_grid, vmem_limit_bytes=VMEM_LIMIT)


def _adaln_kernel(cond_ref, w_ref, b_ref, o_ref):
    x = cond_ref[...]
    a = x * jax.nn.sigmoid(x)
    a_hi = a.astype(BF16)
    a_lo = (a - a_hi.astype(F32)).astype(BF16)
    w = w_ref[...]
    w_hi = w.astype(BF16)
    w_lo = (w - w_hi.astype(F32)).astype(BF16)
    o_ref[...] = _dot(a_hi, w_hi) + _dot(a_lo, w_hi) + _dot(a_hi, w_lo) + b_ref[...]


def _adaln(cond, w_mod, b_mod):
    d, n = w_mod.shape
    bn = n // 8
    return pl.pallas_call(
        _adaln_kernel,
        grid=(n // bn,),
        in_specs=[
            _const_spec((MOD_ROWS, d)),
            pl.BlockSpec((d, bn), lambda j: (0, j)),
            pl.BlockSpec((1, bn), lambda j: (0, j)),
        ],
        out_specs=pl.BlockSpec((MOD_ROWS, bn), lambda j: (0, j)),
        out_shape=jax.ShapeDtypeStruct((MOD_ROWS, n), F32),
        compiler_params=_params(1),
        name="adaln",
    )(cond, w_mod, b_mod.reshape(1, n))


def _ffn_kernel(x_ref, mod_ref, pre_ref, post_ref, wg_ref, wu_ref, wd_ref, o_ref, a_ref, *, mod_base):
    x = x_ref[...]
    shift = mod_ref[0, mod_base:mod_base + 1, :]
    scale = mod_ref[0, mod_base + 1:mod_base + 2, :]
    gate = mod_ref[0, mod_base + 2:mod_base + 3, :]
    xm = (_rms(x, pre_ref[...]) * (1.0 + scale) + shift).astype(BF16)
    d_ff = wg_ref.shape[1]
    for c in range(d_ff // FF_CHUNK):
        sl = slice(c * FF_CHUNK, (c + 1) * FF_CHUNK)
        g = _dot(xm, wg_ref[:, sl])
        u = _dot(xm, wu_ref[:, sl])
        a_ref[:, sl] = (g * jax.nn.sigmoid(g) * u).astype(BF16)
    y = _dot(a_ref[...], wd_ref[...])
    o_ref[...] = x + 0.5 * gate * _rms(y, post_ref[...])


def _ffn(x, mods, pre_g, post_g, wg, wu, wd, *, mod_base, mod_row):
    rows, d = x.shape
    d_ff = wg.shape[1]
    tm = min(ROW_BLOCK, rows)
    return pl.pallas_call(
        functools.partial(_ffn_kernel, mod_base=mod_base),
        grid=(rows // tm,),
        in_specs=[
            pl.BlockSpec((tm, d), lambda i: (i, 0)),
            pl.BlockSpec((1, N_MOD, d), lambda i: (mod_row(i), 0, 0)),
            _const_spec((1, d)),
            _const_spec((1, d)),
            _const_spec((d, d_ff)),
            _const_spec((d, d_ff)),
            _const_spec((d_ff, d)),
        ],
        out_specs=pl.BlockSpec((tm, d), lambda i: (i, 0)),
        out_shape=jax.ShapeDtypeStruct((rows, d), F32),
        scratch_shapes=[pltpu.VMEM((tm, d_ff), BF16)],
        compiler_params=_params(1),
        name="ffn",
    )(x, mods, pre_g, post_g, wg, wu, wd)


def _ctx_kv_kernel(x_ref, mod_ref, g_ref, wk_ref, wv_ref, k_ref, vt_ref):
    x = x_ref[...]
    shift = mod_ref[0, 3:4, :]
    scale = mod_ref[0, 4:5, :]
    xm = (_rms(x, g_ref[...]) * (1.0 + scale) + shift).astype(BF16)
    k = _dot(xm, wk_ref[...])
    v = _dot(xm, wv_ref[...])
    for h in range(N_HEADS):
        sl = slice(h * LANES, (h + 1) * LANES)
        k_ref[0, h] = k[:, sl].astype(BF16)
        vt_ref[0, h, 0] = v[:, sl].T.astype(BF16)


def _ctx_kv(hc, mods, g, w_in, *, batch, ctx_len, mod_row):
    d = hc.shape[1]
    assert ctx_len == K_CHUNK
    return pl.pallas_call(
        _ctx_kv_kernel,
        grid=(batch,),
        in_specs=[
            pl.BlockSpec((ctx_len, d), lambda b: (b, 0)),
            pl.BlockSpec((1, N_MOD, d), lambda b: (mod_row, 0, 0)),
            _const_spec((1, d)),
            pl.BlockSpec((d, d), lambda b: (0, 1), pipeline_mode=pl.Buffered(1)),
            pl.BlockSpec((d, d), lambda b: (0, 2), pipeline_mode=pl.Buffered(1)),
        ],
        out_specs=[
            pl.BlockSpec((1, N_HEADS, ctx_len, LANES), lambda b: (b, 0, 0, 0)),
            pl.BlockSpec((1, N_HEADS, 1, LANES, ctx_len), lambda b: (b, 0, 0, 0, 0)),
        ],
        out_shape=[
            jax.ShapeDtypeStruct((batch, N_HEADS, ctx_len, LANES), BF16),
            jax.ShapeDtypeStruct((batch, N_HEADS, 1, LANES, ctx_len), BF16),
        ],
        compiler_params=_params(1),
        name="ctx_kv",
    )(hc, mods, g, w_in, w_in)


def _rope_tables(n):
    t = jnp.arange(n, dtype=jnp.int32)
    row = (t // GRID_W).astype(F32)
    col = (t % GRID_W).astype(F32)
    inv_freq = ROPE_BASE ** (-2.0 * jnp.arange(N_FREQ, dtype=F32) / AXIS_DIM)
    ang_r = row[:, None] * inv_freq
    ang_c = col[:, None] * inv_freq
    ang = jnp.concatenate([ang_r, ang_r, ang_c, ang_c] * 2, axis=-1)
    cos, sin = jnp.cos(ang), jnp.sin(ang)
    first_half = (jnp.arange(LANES) % AXIS_DIM) < N_FREQ
    sin_up = jnp.where(first_half, -sin, 0.0)
    sin_dn = jnp.where(first_half, 0.0, sin)
    return cos, sin_up, sin_dn


def _proj_kernel(x_ref, mod_ref, g_ref, w_ref, cos_ref, sup_ref, sdn_ref,
                 qt_ref, k_ref, vt_ref, z_ref, gb_ref, sga_ref, sgb_ref):
    tm, d = x_ref.shape
    x = x_ref[...]
    shift = mod_ref[0, 3:4, :]
    scale = mod_ref[0, 4:5, :]
    xm = (_rms(x, g_ref[...]) * (1.0 + scale) + shift).astype(BF16)
    cos, sup, sdn = cos_ref[...], sup_ref[...], sdn_ref[...]

    def group(j):
        return _dot(xm, w_ref[:, j * d:(j + 1) * d])

    def rope(xh):
        return (xh * cos + pltpu.roll(xh, LANES - N_FREQ, axis=1) * sup
                + pltpu.roll(xh, N_FREQ, axis=1) * sdn)

    uq = group(0)
    for h in range(N_HEADS):
        qh = rope(uq[:, h * LANES:(h + 1) * LANES]) * Q_SCALE
        for r in range(tm // Q_BLOCK):
            qt_ref[0, h, r] = qh[r * Q_BLOCK:(r + 1) * Q_BLOCK, :].T.astype(BF16)
    uk = group(1)
    for h in range(N_HEADS):
        k_ref[0, h] = rope(uk[:, h * LANES:(h + 1) * LANES]).astype(BF16)
    uv = group(2)
    for h in range(N_HEADS):
        vh = uv[:, h * LANES:(h + 1) * LANES]
        for r in range(tm // K_CHUNK):
            vt_ref[0, h, r] = vh[r * K_CHUNK:(r + 1) * K_CHUNK, :].T.astype(BF16)
    gb_ref[...] = group(3).astype(BF16)
    z_ref[...] = (group(4) * group(5)).astype(BF16)
    sga_ref[...] = jax.nn.sigmoid(group(6)).astype(BF16)
    sgb_ref[...] = jax.nn.sigmoid(group(7)).astype(BF16)


def _proj(h, mods, g, w_in, tables, *, batch, seq):
    rows, d = h.shape
    tm = ROW_BLOCK
    bpb = seq // tm
    qpb, cpb = tm // Q_BLOCK, tm // K_CHUNK
    row_spec = pl.BlockSpec((tm, d), lambda i: (i, 0))
    row_out = jax.ShapeDtypeStruct((rows, d), BF16)
    tab_spec = pl.BlockSpec((tm, LANES), lambda i: (i % bpb, 0))
    return pl.pallas_call(
        _proj_kernel,
        grid=(rows // tm,),
        in_specs=[
            row_spec,
            pl.BlockSpec((1, N_MOD, d), lambda i: (i // bpb, 0, 0)),
            _const_spec((1, d)),
            _const_spec(w_in.shape),
            tab_spec, tab_spec, tab_spec,
        ],
        out_specs=[
            pl.BlockSpec((1, N_HEADS, qpb, LANES, Q_BLOCK), lambda i: (i // bpb, 0, i % bpb, 0, 0)),
            pl.BlockSpec((1, N_HEADS, tm, LANES), lambda i: (i // bpb, 0, i % bpb, 0)),
            pl.BlockSpec((1, N_HEADS, cpb, LANES, K_CHUNK), lambda i: (i // bpb, 0, i % bpb, 0, 0)),
            row_spec, row_spec, row_spec, row_spec,
        ],
        out_shape=[
            jax.ShapeDtypeStruct((batch, N_HEADS, seq // Q_BLOCK, LANES, Q_BLOCK), BF16),
            jax.ShapeDtypeStruct((batch, N_HEADS, seq, LANES), BF16),
            jax.ShapeDtypeStruct((batch, N_HEADS, seq // K_CHUNK, LANES, K_CHUNK), BF16),
            row_out, row_out, row_out, row_out,
        ],
        compiler_params=_params(1),
        name="proj",
    )(h, mods, g, w_in, *tables)


def _attn_kernel(lam_ref, g_ref, qt_ref, k_ref, vt_ref, o_ref, s0_ref, s1_ref, *, n_qblocks, n_chunks):
    tq = Q_BLOCK
    two_tq = 2 * tq
    s_refs = (s0_ref, s1_ref)

    lv = lam_ref[...]
    lam = (jnp.exp(jnp.sum(lv[0:1] * lv[1:2], axis=-1, keepdims=True))
           - jnp.exp(jnp.sum(lv[2:3] * lv[3:4], axis=-1, keepdims=True)) + LAM_INIT)

    def load_qq(i):
        qt = qt_ref[0, 0, i]
        feat = lax.broadcasted_iota(jnp.int32, qt.shape, 0)
        zero = jnp.zeros_like(qt)
        return jnp.concatenate(
            [jnp.where(feat < HEAD_DIM, qt, zero), jnp.where(feat >= HEAD_DIM, qt, zero)], axis=1)

    def fold(x, op):
        return op(x.reshape(K_CHUNK // SUBLANES, SUBLANES, two_tq), axis=0)

    def stage_a(j, s_ref, qq, mx):
        s = _dot(k_ref[0, 0, j * K_CHUNK:(j + 1) * K_CHUNK, :], qq)
        s_ref[j] = s
        return jnp.maximum(mx, fold(s, jnp.max))

    def stage_b(j, s_ref, m, ls, acc):
        e = jnp.exp2(s_ref[j] - m)
        ls = ls + fold(e, jnp.sum)
        acc = acc + _dot(vt_ref[0, 0, j], e.astype(BF16))
        return ls, acc

    mx0 = jnp.full((SUBLANES, two_tq), -jnp.inf, F32)
    ls0 = jnp.zeros((SUBLANES, two_tq), F32)
    acc0 = jnp.zeros((LANES, two_tq), F32)

    def finalize(i, ls, acc):
        l = jnp.sum(ls, axis=0, keepdims=True)
        o = acc[:, :tq] * (1.0 / l[:, :tq]) - acc[:, tq:] * (lam / l[:, tq:])
        on = o * lax.rsqrt(jnp.mean(o * o, axis=0, keepdims=True) + EPS) * g_ref[...] * (1.0 - LAM_INIT)
        off = pl.multiple_of(i * tq, tq)
        o_ref[0, 0, pl.ds(off, tq), :] = on.T.astype(BF16)


    def run_a(i, parity):
        qq = load_qq(i)
        mx = mx0
        for j in range(n_chunks):
            mx = stage_a(j, s_refs[parity], qq, mx)
        return jnp.max(mx, axis=0, keepdims=True)

    def run_b(parity, m):
        ls, acc = ls0, acc0
        for j in range(n_chunks):
            ls, acc = stage_b(j, s_refs[parity], m, ls, acc)
        return ls, acc

    def run_ab(i, parity, m_prev):
        qq = load_qq(i)
        mx, ls, acc = mx0, ls0, acc0
        for j in range(n_chunks):
            mx = stage_a(j, s_refs[parity], qq, mx)
            ls, acc = stage_b(j, s_refs[1 - parity], m_prev, ls, acc)
        return jnp.max(mx, axis=0, keepdims=True), ls, acc

    def pair(p, carry):
        m, ls, acc = carry
        i = 2 * p + 2
        finalize(i - 2, ls, acc)
        m, ls, acc = run_ab(i, 0, m)
        finalize(i - 1, ls, acc)
        return run_ab(i + 1, 1, m)

    assert n_qblocks % 2 == 0
    m = run_a(0, 0)
    carry = run_ab(1, 1, m)
    m, ls, acc = lax.fori_loop(0, n_qblocks // 2 - 1, pair, carry)
    finalize(n_qblocks - 2, ls, acc)
    finalize(n_qblocks - 1, *run_b(1, m))


def _attn(lam_vecs, subln_g, qt, k_all, vt_all):
    batch, heads, n_qblocks = qt.shape[:3]
    n_keys = k_all.shape[2]
    n_chunks = n_keys // K_CHUNK
    seq = n_qblocks * Q_BLOCK
    return pl.pallas_call(
        functools.partial(_attn_kernel, n_qblocks=n_qblocks, n_chunks=n_chunks),
        grid=(batch, heads),
        in_specs=[
            _const_spec(lam_vecs.shape),
            _const_spec(subln_g.shape),
            pl.BlockSpec((1, 1, n_qblocks, LANES, Q_BLOCK), lambda b, h: (b, h, 0, 0, 0)),
            pl.BlockSpec((1, 1, n_keys, LANES), lambda b, h: (b, h, 0, 0)),
            pl.BlockSpec((1, 1, n_chunks, LANES, K_CHUNK), lambda b, h: (b, h, 0, 0, 0)),
        ],
        out_specs=pl.BlockSpec((1, 1, seq, LANES), lambda b, h: (b, h, 0, 0)),
        out_shape=jax.ShapeDtypeStruct((batch, heads, seq, LANES), BF16),
        scratch_shapes=[pltpu.VMEM((n_chunks, K_CHUNK, 2 * Q_BLOCK), F32)] * 2,
        compiler_params=_params(2),
        name="attn",
    )(lam_vecs, subln_g, qt, k_all, vt_all)


def _mixout_kernel(h_ref, mod_ref, o_ref, z_ref, zp_ref, zn_ref, gb_ref, sga_ref, sgb_ref,
                   cw_ref, wpa_ref, wpb_ref, wo_ref, post_ref, out_ref, *, blocks_per_seq):
    tm = h_ref.shape[0]
    i = pl.program_id(0)
    o = jnp.concatenate([o_ref[0, h] for h in range(N_HEADS)], axis=1)
    y_att = _dot(o, wpa_ref[...])

    z = z_ref[...].astype(F32)
    pos = i % blocks_per_seq
    zp = zp_ref[...].astype(F32)
    zn = zn_ref[...].astype(F32)
    prev_row = jnp.where(pos == 0, 0.0, zp[zp.shape[0] - 1:, :])
    next_row = jnp.where(pos == blocks_per_seq - 1, 0.0, zn[0:1, :])
    rows = lax.broadcasted_iota(jnp.int32, z.shape, 0)
    z_prev = jnp.where(rows == 0, prev_row, pltpu.roll(z, 1, axis=0))
    z_next = jnp.where(rows == tm - 1, next_row, pltpu.roll(z, tm - 1, axis=0))
    conv = z_prev * cw_ref[0:1, :] + z * cw_ref[1:2, :] + z_next * cw_ref[2:3, :]
    y_conv = _dot((gb_ref[...].astype(F32) * conv).astype(BF16), wpb_ref[...])

    merged = sga_ref[...].astype(F32) * y_att + sgb_ref[...].astype(F32) * y_conv
    y = _dot(merged.astype(BF16), wo_ref[...])
    out_ref[...] = h_ref[...] + mod_ref[0, 5:6, :] * _rms(y, post_ref[...])


def _mixout(h, mods, o, z, gb, sga, sgb, conv_w, wpa, wpb, wo, post_g, *, seq):
    rows, d = h.shape
    tm = ROW_BLOCK
    bpb = seq // tm
    halo = 16
    hpb = tm // halo
    n_halo = rows // halo
    row_spec = pl.BlockSpec((tm, d), lambda i: (i, 0))
    return pl.pallas_call(
        functools.partial(_mixout_kernel, blocks_per_seq=bpb),
        grid=(rows // tm,),
        in_specs=[
            row_spec,
            pl.BlockSpec((1, N_MOD, d), lambda i: (i // bpb, 0, 0)),
            pl.BlockSpec((1, N_HEADS, tm, LANES), lambda i: (i // bpb, 0, i % bpb, 0)),
            row_spec,
            pl.BlockSpec((halo, d), lambda i: (jnp.maximum(i * hpb - 1, 0), 0)),
            pl.BlockSpec((halo, d), lambda i: (jnp.minimum((i + 1) * hpb, n_halo - 1), 0)),
            row_spec, row_spec, row_spec,
            _const_spec(conv_w.shape),
            _const_spec(wpa.shape), _const_spec(wpb.shape), _const_spec(wo.shape),
            _const_spec((1, d)),
        ],
        out_specs=row_spec,
        out_shape=jax.ShapeDtypeStruct((rows, d), F32),
        compiler_params=_params(1),
        name="mixout",
    )(h, mods, o, z, z, z, gb, sga, sgb, conv_w, wpa, wpb, wo, post_g)


def kernel(x, c, ctx, c_ctx, w_mod, b_mod, ffn1_pre_g, ffn1_post_g, ffn1_w_gate, ffn1_w_up, ffn1_w_down, mix_pre_g, mix_post_g, w_in, lam_q1, lam_k1, lam_q2, lam_k2, attn_subln_g, conv_w, w_attn_proj, w_conv_proj, w_out, ffn2_pre_g, ffn2_post_g, ffn2_w_gate, ffn2_w_up, ffn2_w_down):
    batch, seq, d = x.shape
    ctx_len = ctx.shape[1]
    assert w_mod.shape[0] == 1, "single-layer stack only"
    assert batch + 1 <= MOD_ROWS and seq % ROW_BLOCK == 0 and seq % GRID_W == 0
    bpb = seq // ROW_BLOCK

    cond = jnp.zeros((MOD_ROWS, d), F32).at[:batch].set(c).at[batch].set(c_ctx)
    mods = _adaln(cond, w_mod[0], b_mod[0]).reshape(MOD_ROWS, N_MOD, d)

    def row(v):
        return v.reshape(1, -1)

    ffn1_w = [w[0].astype(BF16) for w in (ffn1_w_gate, ffn1_w_up, ffn1_w_down)]
    ffn2_w = [w[0].astype(BF16) for w in (ffn2_w_gate, ffn2_w_up, ffn2_w_down)]
    w_in_b = w_in[0].astype(BF16)

    h = _ffn(x.reshape(batch * seq, d), mods, row(ffn1_pre_g), row(ffn1_post_g), *ffn1_w,
             mod_base=0, mod_row=lambda i: i // bpb)
    hc = _ffn(ctx.reshape(batch * ctx_len, d), mods, row(ffn1_pre_g), row(ffn1_post_g), *ffn1_w,
              mod_base=0, mod_row=lambda i: batch)

    kc, vct = _ctx_kv(hc, mods, row(mix_pre_g), w_in_b, batch=batch, ctx_len=ctx_len, mod_row=batch)
    qt, k, vt, z, gb, sga, sgb = _proj(h, mods, row(mix_pre_g), w_in_b, _rope_tables(seq),
                                       batch=batch, seq=seq)
    k_all = jnp.concatenate([k, kc], axis=2)
    vt_all = jnp.concatenate([vt, vct], axis=2)
    lam_vecs = jnp.concatenate([lam_q1, lam_k1, lam_q2, lam_k2], axis=0)
    o = _attn(lam_vecs, attn_subln_g.reshape(V_DIM, 1), qt, k_all, vt_all)
    h = _mixout(h, mods, o, z, gb, sga, sgb, conv_w[0], w_attn_proj[0].astype(BF16),
                w_conv_proj[0].astype(BF16), w_out[0].astype(BF16), row(mix_post_g), seq=seq)
    h = _ffn(h, mods, row(ffn2_pre_g), row(ffn2_post_g), *ffn2_w,
             mod_base=6, mod_row=lambda i: i // bpb)
    return h.reshape(batch, seq, d)
```

```python
import functools
import math

import jax
import jax.numpy as jnp
from jax import lax
from jax.experimental import pallas as pl
from jax.experimental.pallas import tpu as pltpu

F32 = jnp.float32
BF16 = jnp.bfloat16

GRID_W = 64
N_HEADS = 8
HEAD_DIM = 64
V_DIM = 2 * HEAD_DIM
CONV_K = 3
N_MOD = 9
ROPE_BASE = 10000.0
AXIS_DIM = HEAD_DIM // 2
N_FREQ = AXIS_DIM // 2
EPS = 1e-6
LAM_INIT = 0.8 - 0.6 * math.exp(-0.3 * 0)
Q_SCALE = (HEAD_DIM ** -0.5) * math.log2(math.e)

LANES = 128
SUBLANES = 8
MOD_ROWS = 8
VMEM_LIMIT = 56 * 1024 * 1024

ROW_BLOCK = 512
FFN_ROW_BLOCK = 1024
FF_CHUNK = 256
Q_BLOCK = 128
K_CHUNK = 256


def _dot(a, b):
    return jnp.dot(a, b, preferred_element_type=F32)


def _const_spec(shape):
    return pl.BlockSpec(shape, lambda *_: (0,) * len(shape), pipeline_mode=pl.Buffered(1))


def _rms(x, g):
    return x * lax.rsqrt(jnp.mean(x * x, axis=-1, keepdims=True) + EPS) * g


def _params(n_grid):
    return pltpu.CompilerParams(
        dimension_semantics=("arbitrary",) * n_grid, vmem_limit_bytes=VMEM_LIMIT)


def _adaln_kernel(cond_ref, w_ref, b_ref, o_ref):
    x = cond_ref[...]
    a = x * jax.nn.sigmoid(x)
    a_hi = a.astype(BF16)
    a_lo = (a - a_hi.astype(F32)).astype(BF16)
    w = w_ref[...]
    w_hi = w.astype(BF16)
    w_lo = (w - w_hi.astype(F32)).astype(BF16)
    o_ref[...] = _dot(a_hi, w_hi) + _dot(a_lo, w_hi) + _dot(a_hi, w_lo) + b_ref[...]


def _adaln(cond, w_mod, b_mod):
    d, n = w_mod.shape
    bn = n // 8
    return pl.pallas_call(
        _adaln_kernel,
        grid=(n // bn,),
        in_specs=[
            _const_spec((MOD_ROWS, d)),
            pl.BlockSpec((d, bn), lambda j: (0, j)),
            pl.BlockSpec((1, bn), lambda j: (0, j)),
        ],
        out_specs=pl.BlockSpec((MOD_ROWS, bn), lambda j: (0, j)),
        out_shape=jax.ShapeDtypeStruct((MOD_ROWS, n), F32),
        compiler_params=_params(1),
        name="adaln",
    )(cond, w_mod, b_mod.reshape(1, n))


def _ffn_kernel(x_ref, mod_ref, pre_ref, post_ref, wg_ref, wu_ref, wd_ref, o_ref, a_ref, *, mod_base):
    x = x_ref[...]
    shift = mod_ref[0, mod_base:mod_base + 1, :]
    scale = mod_ref[0, mod_base + 1:mod_base + 2, :]
    gate = mod_ref[0, mod_base + 2:mod_base + 3, :]
    xm = (_rms(x, pre_ref[...]) * (1.0 + scale) + shift).astype(BF16)
    d_ff = wg_ref.shape[1]
    for c in range(d_ff // FF_CHUNK):
        sl = slice(c * FF_CHUNK, (c + 1) * FF_CHUNK)
        g = _dot(xm, wg_ref[:, sl])
        u = _dot(xm, wu_ref[:, sl])
        a_ref[:, sl] = (g * jax.nn.sigmoid(g) * u).astype(BF16)
    y = _dot(a_ref[...], wd_ref[...])
    o_ref[...] = x + 0.5 * gate * _rms(y, post_ref[...])


def _ffn(x, mods, pre_g, post_g, wg, wu, wd, *, mod_base, mod_row0, rows_per_mod_row):
    rows, d = x.shape
    d_ff = wg.shape[1]
    tm = min(FFN_ROW_BLOCK, rows)
    assert rows_per_mod_row % tm == 0
    bpm = rows_per_mod_row // tm
    return pl.pallas_call(
        functools.partial(_ffn_kernel, mod_base=mod_base),
        grid=(rows // tm,),
        in_specs=[
            pl.BlockSpec((tm, d), lambda i: (i, 0)),
            pl.BlockSpec((1, N_MOD, d), lambda i: (mod_row0 + i // bpm, 0, 0)),
            _const_spec((1, d)),
            _const_spec((1, d)),
            _const_spec((d, d_ff)),
            _const_spec((d, d_ff)),
            _const_spec((d_ff, d)),
        ],
        out_specs=pl.BlockSpec((tm, d), lambda i: (i, 0)),
        out_shape=jax.ShapeDtypeStruct((rows, d), F32),
        scratch_shapes=[pltpu.VMEM((tm, d_ff), BF16)],
        compiler_params=_params(1),
        name="ffn",
    )(x, mods, pre_g, post_g, wg, wu, wd)


def _ctx_kv_kernel(x_ref, mod_ref, g_ref, wk_ref, wv_ref, k_in_ref, vt_in_ref, k_ref, vt_ref):
    del k_in_ref, vt_in_ref
    x = x_ref[...]
    shift = mod_ref[0, 3:4, :]
    scale = mod_ref[0, 4:5, :]
    xm = (_rms(x, g_ref[...]) * (1.0 + scale) + shift).astype(BF16)
    k = _dot(xm, wk_ref[...])
    v = _dot(xm, wv_ref[...])
    for h in range(N_HEADS):
        sl = slice(h * LANES, (h + 1) * LANES)
        k_ref[0, h] = k[:, sl].astype(BF16)
        vt_ref[0, h, 0] = v[:, sl].T.astype(BF16)


def _ctx_kv(hc, mods, g, w_in, k_all, vt_all, *, batch, ctx_len, mod_row):
    d = hc.shape[1]
    assert ctx_len == K_CHUNK
    last = vt_all.shape[2] - 1
    return pl.pallas_call(
        _ctx_kv_kernel,
        grid=(batch,),
        in_specs=[
            pl.BlockSpec((ctx_len, d), lambda b: (b, 0)),
            pl.BlockSpec((1, N_MOD, d), lambda b: (mod_row, 0, 0)),
            _const_spec((1, d)),
            pl.BlockSpec((d, d), lambda b: (0, 1), pipeline_mode=pl.Buffered(1)),
            pl.BlockSpec((d, d), lambda b: (0, 2), pipeline_mode=pl.Buffered(1)),
            pl.BlockSpec(memory_space=pl.ANY),
            pl.BlockSpec(memory_space=pl.ANY),
        ],
        out_specs=[
            pl.BlockSpec((1, N_HEADS, ctx_len, LANES), lambda b: (b, 0, last, 0)),
            pl.BlockSpec((1, N_HEADS, 1, LANES, ctx_len), lambda b: (b, 0, last, 0, 0)),
        ],
        out_shape=[
            jax.ShapeDtypeStruct(k_all.shape, BF16),
            jax.ShapeDtypeStruct(vt_all.shape, BF16),
        ],
        input_output_aliases={5: 0, 6: 1},
        compiler_params=_params(1),
        name="ctx_kv",
    )(hc, mods, g, w_in, w_in, k_all, vt_all)


def _rope_tables(n):
    t = jnp.arange(n, dtype=jnp.int32)
    row = (t // GRID_W).astype(F32)
    col = (t % GRID_W).astype(F32)
    inv_freq = ROPE_BASE ** (-2.0 * jnp.arange(N_FREQ, dtype=F32) / AXIS_DIM)
    ang_r = row[:, None] * inv_freq
    ang_c = col[:, None] * inv_freq
    ang = jnp.concatenate([ang_r, ang_r, ang_c, ang_c] * 2, axis=-1)
    cos, sin = jnp.cos(ang), jnp.sin(ang)
    first_half = (jnp.arange(LANES) % AXIS_DIM) < N_FREQ
    sin_up = jnp.where(first_half, -sin, 0.0)
    sin_dn = jnp.where(first_half, 0.0, sin)
    return cos, sin_up, sin_dn


def _proj_kernel(x_ref, mod_ref, g_ref, w_ref, cos_ref, sup_ref, sdn_ref,
                 qt_ref, k_ref, vt_ref, z_ref, gb_ref, sga_ref, sgb_ref):
    tm, d = x_ref.shape
    x = x_ref[...]
    shift = mod_ref[0, 3:4, :]
    scale = mod_ref[0, 4:5, :]
    xm = (_rms(x, g_ref[...]) * (1.0 + scale) + shift).astype(BF16)
    cos, sup, sdn = cos_ref[...], sup_ref[...], sdn_ref[...]

    def group(j):
        return _dot(xm, w_ref[:, j * d:(j + 1) * d])

    def rope(xh):
        return (xh * cos + pltpu.roll(xh, LANES - N_FREQ, axis=1) * sup
                + pltpu.roll(xh, N_FREQ, axis=1) * sdn)

    uq = group(0)
    for h in range(N_HEADS):
        qh = rope(uq[:, h * LANES:(h + 1) * LANES]) * Q_SCALE
        for r in range(tm // Q_BLOCK):
            qt_ref[0, h, r] = qh[r * Q_BLOCK:(r + 1) * Q_BLOCK, :].T.astype(BF16)
    uk = group(1)
    for h in range(N_HEADS):
        k_ref[0, h] = rope(uk[:, h * LANES:(h + 1) * LANES]).astype(BF16)
    uv = group(2)
    for h in range(N_HEADS):
        vh = uv[:, h * LANES:(h + 1) * LANES]
        for r in range(tm // K_CHUNK):
            vt_ref[0, h, r] = vh[r * K_CHUNK:(r + 1) * K_CHUNK, :].T.astype(BF16)
    gb_ref[...] = group(3).astype(BF16)
    z_ref[...] = (group(4) * group(5)).astype(BF16)
    sga_ref[...] = jax.nn.sigmoid(group(6)).astype(BF16)
    sgb_ref[...] = jax.nn.sigmoid(group(7)).astype(BF16)


def _proj(h, mods, g, w_in, tables, *, batch, seq, n_keys):
    rows, d = h.shape
    tm = ROW_BLOCK
    bpb = seq // tm
    qpb, cpb = tm // Q_BLOCK, tm // K_CHUNK
    row_spec = pl.BlockSpec((tm, d), lambda i: (i, 0))
    row_out = jax.ShapeDtypeStruct((rows, d), BF16)
    tab_spec = pl.BlockSpec((tm, LANES), lambda i: (i % bpb, 0))
    return pl.pallas_call(
        _proj_kernel,
        grid=(rows // tm,),
        in_specs=[
            row_spec,
            pl.BlockSpec((1, N_MOD, d), lambda i: (i // bpb, 0, 0)),
            _const_spec((1, d)),
            _const_spec(w_in.shape),
            tab_spec, tab_spec, tab_spec,
        ],
        out_specs=[
            pl.BlockSpec((1, N_HEADS, qpb, LANES, Q_BLOCK), lambda i: (i // bpb, 0, i % bpb, 0, 0)),
            pl.BlockSpec((1, N_HEADS, tm, LANES), lambda i: (i // bpb, 0, i % bpb, 0)),
            pl.BlockSpec((1, N_HEADS, cpb, LANES, K_CHUNK), lambda i: (i // bpb, 0, i % bpb, 0, 0)),
            row_spec, row_spec, row_spec, row_spec,
        ],
        out_shape=[
            jax.ShapeDtypeStruct((batch, N_HEADS, seq // Q_BLOCK, LANES, Q_BLOCK), BF16),
            jax.ShapeDtypeStruct((batch, N_HEADS, n_keys, LANES), BF16),
            jax.ShapeDtypeStruct((batch, N_HEADS, n_keys // K_CHUNK, LANES, K_CHUNK), BF16),
            row_out, row_out, row_out, row_out,
        ],
        compiler_params=_params(1),
        name="proj",
    )(h, mods, g, w_in, *tables)


def _attn_kernel(lam_ref, g_ref, qt_ref, k_ref, vt_ref, o_ref, s0_ref, s1_ref, *, n_qblocks, n_chunks):
    tq = Q_BLOCK
    two_tq = 2 * tq
    s_refs = (s0_ref, s1_ref)

    lv = lam_ref[...]
    lam = (jnp.exp(jnp.sum(lv[0:1] * lv[1:2], axis=-1, keepdims=True))
           - jnp.exp(jnp.sum(lv[2:3] * lv[3:4], axis=-1, keepdims=True)) + LAM_INIT)

    def load_qq(i):
        qt = qt_ref[0, 0, i]
        feat = lax.broadcasted_iota(jnp.int32, qt.shape, 0)
        zero = jnp.zeros_like(qt)
        return jnp.concatenate(
            [jnp.where(feat < HEAD_DIM, qt, zero), jnp.where(feat >= HEAD_DIM, qt, zero)], axis=1)

    def fold(x, op):
        return op(x.reshape(K_CHUNK // SUBLANES, SUBLANES, two_tq), axis=0)

    def stage_a(j, s_ref, qq, mx):
        s = _dot(k_ref[0, 0, j * K_CHUNK:(j + 1) * K_CHUNK, :], qq)
        s_ref[j] = s
        return jnp.maximum(mx, fold(s, jnp.max))

    def stage_b(j, s_ref, m, ls, acc):
        e = jnp.exp2(s_ref[j] - m)
        ls = ls + fold(e, jnp.sum)
        acc = acc + _dot(vt_ref[0, 0, j], e.astype(BF16))
        return ls, acc

    mx0 = jnp.full((SUBLANES, two_tq), -jnp.inf, F32)
    ls0 = jnp.zeros((SUBLANES, two_tq), F32)
    acc0 = jnp.zeros((LANES, two_tq), F32)

    def finalize(i, ls, acc):
        l = jnp.sum(ls, axis=0, keepdims=True)
        o = acc[:, :tq] * (1.0 / l[:, :tq]) - acc[:, tq:] * (lam / l[:, tq:])
        on = o * lax.rsqrt(jnp.mean(o * o, axis=0, keepdims=True) + EPS) * g_ref[...] * (1.0 - LAM_INIT)
        off = pl.multiple_of(i * tq, tq)
        o_ref[0, 0, pl.ds(off, tq), :] = on.T.astype(BF16)


    def run_a(i, parity):
        qq = load_qq(i)
        mx = mx0
        for j in range(n_chunks):
            mx = stage_a(j, s_refs[parity], qq, mx)
        return jnp.max(mx, axis=0, keepdims=True)

    def run_b(parity, m):
        ls, acc = ls0, acc0
        for j in range(n_chunks):
            ls, acc = stage_b(j, s_refs[parity], m, ls, acc)
        return ls, acc

    def run_ab(i, parity, m_prev):
        qq = load_qq(i)
        mx, ls, acc = mx0, ls0, acc0
        for j in range(n_chunks):
            mx = stage_a(j, s_refs[parity], qq, mx)
            ls, acc = stage_b(j, s_refs[1 - parity], m_prev, ls, acc)
        return jnp.max(mx, axis=0, keepdims=True), ls, acc

    def pair(p, carry):
        m, ls, acc = carry
        i = 2 * p + 2
        finalize(i - 2, ls, acc)
        m, ls, acc = run_ab(i, 0, m)
        finalize(i - 1, ls, acc)
        return run_ab(i + 1, 1, m)

    assert n_qblocks % 2 == 0
    m = run_a(0, 0)
    carry = run_ab(1, 1, m)
    m, ls, acc = lax.fori_loop(0, n_qblocks // 2 - 1, pair, carry)
    finalize(n_qblocks - 2, ls, acc)
    finalize(n_qblocks - 1, *run_b(1, m))


def _attn(lam_vecs, subln_g, qt, k_all, vt_all):
    batch, heads, n_qblocks = qt.shape[:3]
    n_keys = k_all.shape[2]
    n_chunks = n_keys // K_CHUNK
    seq = n_qblocks * Q_BLOCK
    return pl.pallas_call(
        functools.partial(_attn_kernel, n_qblocks=n_qblocks, n_chunks=n_chunks),
        grid=(batch, heads),
        in_specs=[
            _const_spec(lam_vecs.shape),
            _const_spec(subln_g.shape),
            pl.BlockSpec((1, 1, n_qblocks, LANES, Q_BLOCK), lambda b, h: (b, h, 0, 0, 0)),
            pl.BlockSpec((1, 1, n_keys, LANES), lambda b, h: (b, h, 0, 0)),
            pl.BlockSpec((1, 1, n_chunks, LANES, K_CHUNK), lambda b, h: (b, h, 0, 0, 0)),
        ],
        out_specs=pl.BlockSpec((1, 1, seq, LANES), lambda b, h: (b, h, 0, 0)),
        out_shape=jax.ShapeDtypeStruct((batch, heads, seq, LANES), BF16),
        scratch_shapes=[pltpu.VMEM((n_chunks, K_CHUNK, 2 * Q_BLOCK), F32)] * 2,
        compiler_params=_params(2),
        name="attn",
    )(lam_vecs, subln_g, qt, k_all, vt_all)


def _mixout_kernel(h_ref, mod_ref, o_ref, z_ref, zp_ref, zn_ref, gb_ref, sga_ref, sgb_ref,
                   cw_ref, wpa_ref, wpb_ref, wo_ref, post_ref, out_ref, *, blocks_per_seq):
    tm = h_ref.shape[0]
    i = pl.program_id(0)
    o = jnp.concatenate([o_ref[0, h] for h in range(N_HEADS)], axis=1)
    y_att = _dot(o, wpa_ref[...])

    z = z_ref[...].astype(F32)
    pos = i % blocks_per_seq
    zp = zp_ref[...].astype(F32)
    zn = zn_ref[...].astype(F32)
    prev_row = jnp.where(pos == 0, 0.0, zp[zp.shape[0] - 1:, :])
    next_row = jnp.where(pos == blocks_per_seq - 1, 0.0, zn[0:1, :])
    rows = lax.broadcasted_iota(jnp.int32, z.shape, 0)
    z_prev = jnp.where(rows == 0, prev_row, pltpu.roll(z, 1, axis=0))
    z_next = jnp.where(rows == tm - 1, next_row, pltpu.roll(z, tm - 1, axis=0))
    conv = z_prev * cw_ref[0:1, :] + z * cw_ref[1:2, :] + z_next * cw_ref[2:3, :]
    y_conv = _dot((gb_ref[...].astype(F32) * conv).astype(BF16), wpb_ref[...])

    merged = sga_ref[...].astype(F32) * y_att + sgb_ref[...].astype(F32) * y_conv
    y = _dot(merged.astype(BF16), wo_ref[...])
    out_ref[...] = h_ref[...] + mod_ref[0, 5:6, :] * _rms(y, post_ref[...])


def _mixout(h, mods, o, z, gb, sga, sgb, conv_w, wpa, wpb, wo, post_g, *, seq):
    rows, d = h.shape
    tm = ROW_BLOCK
    bpb = seq // tm
    halo = 16
    hpb = tm // halo
    n_halo = rows // halo
    row_spec = pl.BlockSpec((tm, d), lambda i: (i, 0))
    return pl.pallas_call(
        functools.partial(_mixout_kernel, blocks_per_seq=bpb),
        grid=(rows // tm,),
        in_specs=[
            row_spec,
            pl.BlockSpec((1, N_MOD, d), lambda i: (i // bpb, 0, 0)),
            pl.BlockSpec((1, N_HEADS, tm, LANES), lambda i: (i // bpb, 0, i % bpb, 0)),
            row_spec,
            pl.BlockSpec((halo, d), lambda i: (jnp.maximum(i * hpb - 1, 0), 0)),
            pl.BlockSpec((halo, d), lambda i: (jnp.minimum((i + 1) * hpb, n_halo - 1), 0)),
            row_spec, row_spec, row_spec,
            _const_spec(conv_w.shape),
            _const_spec(wpa.shape), _const_spec(wpb.shape), _const_spec(wo.shape),
            _const_spec((1, d)),
        ],
        out_specs=row_spec,
        out_shape=jax.ShapeDtypeStruct((rows, d), F32),
        compiler_params=_params(1),
        name="mixout",
    )(h, mods, o, z, z, z, gb, sga, sgb, conv_w, wpa, wpb, wo, post_g)


def kernel(x, c, ctx, c_ctx, w_mod, b_mod, ffn1_pre_g, ffn1_post_g, ffn1_w_gate, ffn1_w_up, ffn1_w_down, mix_pre_g, mix_post_g, w_in, lam_q1, lam_k1, lam_q2, lam_k2, attn_subln_g, conv_w, w_attn_proj, w_conv_proj, w_out, ffn2_pre_g, ffn2_post_g, ffn2_w_gate, ffn2_w_up, ffn2_w_down):
    batch, seq, d = x.shape
    ctx_len = ctx.shape[1]
    assert w_mod.shape[0] == 1, "single-layer stack only"
    assert batch + 1 <= MOD_ROWS and seq % ROW_BLOCK == 0 and seq % GRID_W == 0

    cond = jnp.zeros((MOD_ROWS, d), F32).at[:batch].set(c).at[batch].set(c_ctx)
    mods = _adaln(cond, w_mod[0], b_mod[0]).reshape(MOD_ROWS, N_MOD, d)

    def row(v):
        return v.reshape(1, -1)

    ffn1_w = [w[0].astype(BF16) for w in (ffn1_w_gate, ffn1_w_up, ffn1_w_down)]
    ffn2_w = [w[0].astype(BF16) for w in (ffn2_w_gate, ffn2_w_up, ffn2_w_down)]
    w_in_b = w_in[0].astype(BF16)

    h = _ffn(x.reshape(batch * seq, d), mods, row(ffn1_pre_g), row(ffn1_post_g), *ffn1_w,
             mod_base=0, mod_row0=0, rows_per_mod_row=seq)
    hc = _ffn(ctx.reshape(batch * ctx_len, d), mods, row(ffn1_pre_g), row(ffn1_post_g), *ffn1_w,
              mod_base=0, mod_row0=batch, rows_per_mod_row=batch * ctx_len)

    qt, k_all, vt_all, z, gb, sga, sgb = _proj(h, mods, row(mix_pre_g), w_in_b, _rope_tables(seq),
                                               batch=batch, seq=seq, n_keys=seq + ctx_len)
    k_all, vt_all = _ctx_kv(hc, mods, row(mix_pre_g), w_in_b, k_all, vt_all,
                            batch=batch, ctx_len=ctx_len, mod_row=batch)
    lam_vecs = jnp.concatenate([lam_q1, lam_k1, lam_q2, lam_k2], axis=0)
    o = _attn(lam_vecs, attn_subln_g.reshape(V_DIM, 1), qt, k_all, vt_all)
    h = _mixout(h, mods, o, z, gb, sga, sgb, conv_w[0], w_attn_proj[0].astype(BF16),
                w_conv_proj[0].astype(BF16), w_out[0].astype(BF16), row(mix_post_g), seq=seq)
    h = _ffn(h, mods, row(ffn2_pre_g), row(ffn2_post_g), *ffn2_w,
             mod_base=6, mod_row0=0, rows_per_mod_row=seq)
    return h.reshape(batch, seq, d)
```

```python
import functools
import math

import jax
import jax.numpy as jnp
from jax import lax
from jax.experimental import pallas as pl
from jax.experimental.pallas import tpu as pltpu

F32 = jnp.float32
BF16 = jnp.bfloat16

GRID_W = 64
N_HEADS = 8
HEAD_DIM = 64
V_DIM = 2 * HEAD_DIM
CONV_K = 3
N_MOD = 9
ROPE_BASE = 10000.0
AXIS_DIM = HEAD_DIM // 2
N_FREQ = AXIS_DIM // 2
EPS = 1e-6
LAM_INIT = 0.8 - 0.6 * math.exp(-0.3 * 0)
Q_SCALE = (HEAD_DIM ** -0.5) * math.log2(math.e)

LANES = 128
SUBLANES = 8
MOD_ROWS = 8
VMEM_LIMIT = 56 * 1024 * 1024

ROW_BLOCK = 512
FFN_ROW_BLOCK = 1024
FF_CHUNK = 256
Q_BLOCK = 128
K_CHUNK = 256


def _dot(a, b):
    return jnp.dot(a, b, preferred_element_type=F32)


def _const_spec(shape):
    return pl.BlockSpec(shape, lambda *_: (0,) * len(shape), pipeline_mode=pl.Buffered(1))


def _rms(x, g):
    return x * lax.rsqrt(jnp.mean(x * x, axis=-1, keepdims=True) + EPS) * g


def _params(n_grid):
    return pltpu.CompilerParams(
        dimension_semantics=("arbitrary",) * n_grid, vmem_limit_bytes=VMEM_LIMIT)


def _adaln_kernel(cond_ref, w_ref, b_ref, o_ref):
    x = cond_ref[...]
    a = x * jax.nn.sigmoid(x)
    a_hi = a.astype(BF16)
    a_lo = (a - a_hi.astype(F32)).astype(BF16)
    w = w_ref[...]
    w_hi = w.astype(BF16)
    w_lo = (w - w_hi.astype(F32)).astype(BF16)
    o_ref[...] = _dot(a_hi, w_hi) + _dot(a_lo, w_hi) + _dot(a_hi, w_lo) + b_ref[...]


def _adaln(cond, w_mod, b_mod):
    d, n = w_mod.shape
    bn = n // 8
    return pl.pallas_call(
        _adaln_kernel,
        grid=(n // bn,),
        in_specs=[
            _const_spec((MOD_ROWS, d)),
            pl.BlockSpec((d, bn), lambda j: (0, j)),
            pl.BlockSpec((1, bn), lambda j: (0, j)),
        ],
        out_specs=pl.BlockSpec((MOD_ROWS, bn), lambda j: (0, j)),
        out_shape=jax.ShapeDtypeStruct((MOD_ROWS, n), F32),
        compiler_params=_params(1),
        name="adaln",
    )(cond, w_mod, b_mod.reshape(1, n))


def _ffn_kernel(x_ref, mod_ref, pre_ref, post_ref, wg_ref, wu_ref, wd_ref, o_ref, a_ref, *, mod_base):
    x = x_ref[...]
    shift = mod_ref[0, mod_base:mod_base + 1, :]
    scale = mod_ref[0, mod_base + 1:mod_base + 2, :]
    gate = mod_ref[0, mod_base + 2:mod_base + 3, :]
    xm = (_rms(x, pre_ref[...]) * (1.0 + scale) + shift).astype(BF16)
    d_ff = wg_ref.shape[1]
    for c in range(d_ff // FF_CHUNK):
        sl = slice(c * FF_CHUNK, (c + 1) * FF_CHUNK)
        g = _dot(xm, wg_ref[:, sl])
        u = _dot(xm, wu_ref[:, sl])
        a_ref[:, sl] = (g * jax.nn.sigmoid(g) * u).astype(BF16)
    y = _dot(a_ref[...], wd_ref[...])
    o_ref[...] = x + 0.5 * gate * _rms(y, post_ref[...])


def _ffn(x, mods, pre_g, post_g, wg, wu, wd, *, mod_base, mod_row0, rows_per_mod_row):
    rows, d = x.shape
    d_ff = wg.shape[1]
    tm = min(FFN_ROW_BLOCK, rows)
    assert rows_per_mod_row % tm == 0
    bpm = rows_per_mod_row // tm
    return pl.pallas_call(
        functools.partial(_ffn_kernel, mod_base=mod_base),
        grid=(rows // tm,),
        in_specs=[
            pl.BlockSpec((tm, d), lambda i: (i, 0)),
            pl.BlockSpec((1, N_MOD, d), lambda i: (mod_row0 + i // bpm, 0, 0)),
            _const_spec((1, d)),
            _const_spec((1, d)),
            _const_spec((d, d_ff)),
            _const_spec((d, d_ff)),
            _const_spec((d_ff, d)),
        ],
        out_specs=pl.BlockSpec((tm, d), lambda i: (i, 0)),
        out_shape=jax.ShapeDtypeStruct((rows, d), F32),
        scratch_shapes=[pltpu.VMEM((tm, d_ff), BF16)],
        compiler_params=_params(1),
        name="ffn",
    )(x, mods, pre_g, post_g, wg, wu, wd)


def _ctx_kv_kernel(x_ref, mod_ref, g_ref, wk_ref, wv_ref, k_in_ref, vt_in_ref, k_ref, vt_ref):
    del k_in_ref, vt_in_ref
    x = x_ref[...]
    shift = mod_ref[0, 3:4, :]
    scale = mod_ref[0, 4:5, :]
    xm = (_rms(x, g_ref[...]) * (1.0 + scale) + shift).astype(BF16)
    k = _dot(xm, wk_ref[...])
    v = _dot(xm, wv_ref[...])
    for h in range(N_HEADS):
        sl = slice(h * LANES, (h + 1) * LANES)
        k_ref[0, h] = k[:, sl].astype(BF16)
        vt_ref[0, h, 0] = v[:, sl].T.astype(BF16)


def _ctx_kv(hc, mods, g, w_in, k_all, vt_all, *, batch, ctx_len, mod_row):
    d = hc.shape[1]
    assert ctx_len == K_CHUNK
    last = vt_all.shape[2] - 1
    return pl.pallas_call(
        _ctx_kv_kernel,
        grid=(batch,),
        in_specs=[
            pl.BlockSpec((ctx_len, d), lambda b: (b, 0)),
            pl.BlockSpec((1, N_MOD, d), lambda b: (mod_row, 0, 0)),
            _const_spec((1, d)),
            pl.BlockSpec((d, d), lambda b: (0, 1), pipeline_mode=pl.Buffered(1)),
            pl.BlockSpec((d, d), lambda b: (0, 2), pipeline_mode=pl.Buffered(1)),
            pl.BlockSpec(memory_space=pl.ANY),
            pl.BlockSpec(memory_space=pl.ANY),
        ],
        out_specs=[
            pl.BlockSpec((1, N_HEADS, ctx_len, LANES), lambda b: (b, 0, last, 0)),
            pl.BlockSpec((1, N_HEADS, 1, LANES, ctx_len), lambda b: (b, 0, last, 0, 0)),
        ],
        out_shape=[
            jax.ShapeDtypeStruct(k_all.shape, BF16),
            jax.ShapeDtypeStruct(vt_all.shape, BF16),
        ],
        input_output_aliases={5: 0, 6: 1},
        compiler_params=_params(1),
        name="ctx_kv",
    )(hc, mods, g, w_in, w_in, k_all, vt_all)


def _rope_tables(n_rows, tm):
    inv_freq = ROPE_BASE ** (-2.0 * jnp.arange(N_FREQ, dtype=F32) / AXIS_DIM)
    lane = jnp.arange(LANES)
    row_lane = (lane % HEAD_DIM) < AXIS_DIM
    first_half = (lane % AXIS_DIM) < N_FREQ

    def factors(pos, keep):
        ang = jnp.tile(pos.astype(F32)[:, None] * inv_freq, (1, LANES // N_FREQ))
        cos, sin = jnp.cos(ang), jnp.sin(ang)
        tabs = (cos, jnp.where(first_half, -sin, 0.0), jnp.where(first_half, 0.0, sin))
        return [jnp.where(keep, t, 0.0) for t in tabs]

    return (factors(jnp.arange(n_rows), row_lane)
            + factors(jnp.arange(tm) % GRID_W, jnp.logical_not(row_lane)))


def _proj_kernel(x_ref, mod_ref, g_ref, w_ref, rcos_ref, rsup_ref, rsdn_ref, ccos_ref, csup_ref, csdn_ref,
                 qt_ref, k_ref, vt_ref, z_ref, gb_ref, sga_ref, sgb_ref):
    tm, d = x_ref.shape
    x = x_ref[...]
    shift = mod_ref[0, 3:4, :]
    scale = mod_ref[0, 4:5, :]
    xm = (_rms(x, g_ref[...]) * (1.0 + scale) + shift).astype(BF16)

    def table(row_ref, col_ref):
        rt = row_ref[...]
        rows = [jnp.broadcast_to(rt[r:r + 1, :], (GRID_W, LANES)) for r in range(tm // GRID_W)]
        return jnp.concatenate(rows, axis=0) + col_ref[...]

    cos, sup, sdn = table(rcos_ref, ccos_ref), table(rsup_ref, csup_ref), table(rsdn_ref, csdn_ref)

    def group(j):
        return _dot(xm, w_ref[:, j * d:(j + 1) * d])

    def rope(xh):
        return (xh * cos + pltpu.roll(xh, LANES - N_FREQ, axis=1) * sup
                + pltpu.roll(xh, N_FREQ, axis=1) * sdn)

    uq = group(0)
    for h in range(N_HEADS):
        qh = rope(uq[:, h * LANES:(h + 1) * LANES]) * Q_SCALE
        for r in range(tm // Q_BLOCK):
            qt_ref[0, h, r] = qh[r * Q_BLOCK:(r + 1) * Q_BLOCK, :].T.astype(BF16)
    uk = group(1)
    for h in range(N_HEADS):
        k_ref[0, h] = rope(uk[:, h * LANES:(h + 1) * LANES]).astype(BF16)
    uv = group(2)
    for h in range(N_HEADS):
        vh = uv[:, h * LANES:(h + 1) * LANES]
        for r in range(tm // K_CHUNK):
            vt_ref[0, h, r] = vh[r * K_CHUNK:(r + 1) * K_CHUNK, :].T.astype(BF16)
    gb_ref[...] = group(3).astype(BF16)
    z_ref[...] = (group(4) * group(5)).astype(BF16)
    sga_ref[...] = jax.nn.sigmoid(group(6)).astype(BF16)
    sgb_ref[...] = jax.nn.sigmoid(group(7)).astype(BF16)


def _proj(h, mods, g, w_in, tables, *, batch, seq, n_keys):
    rows, d = h.shape
    tm = ROW_BLOCK
    bpb = seq // tm
    qpb, cpb = tm // Q_BLOCK, tm // K_CHUNK
    row_spec = pl.BlockSpec((tm, d), lambda i: (i, 0))
    row_out = jax.ShapeDtypeStruct((rows, d), BF16)
    row_tab = pl.BlockSpec((tm // GRID_W, LANES), lambda i: (i % bpb, 0))
    col_tab = _const_spec((tm, LANES))
    return pl.pallas_call(
        _proj_kernel,
        grid=(rows // tm,),
        in_specs=[
            row_spec,
            pl.BlockSpec((1, N_MOD, d), lambda i: (i // bpb, 0, 0)),
            _const_spec((1, d)),
            _const_spec(w_in.shape),
            row_tab, row_tab, row_tab, col_tab, col_tab, col_tab,
        ],
        out_specs=[
            pl.BlockSpec((1, N_HEADS, qpb, LANES, Q_BLOCK), lambda i: (i // bpb, 0, i % bpb, 0, 0)),
            pl.BlockSpec((1, N_HEADS, tm, LANES), lambda i: (i // bpb, 0, i % bpb, 0)),
            pl.BlockSpec((1, N_HEADS, cpb, LANES, K_CHUNK), lambda i: (i // bpb, 0, i % bpb, 0, 0)),
            row_spec, row_spec, row_spec, row_spec,
        ],
        out_shape=[
            jax.ShapeDtypeStruct((batch, N_HEADS, seq // Q_BLOCK, LANES, Q_BLOCK), BF16),
            jax.ShapeDtypeStruct((batch, N_HEADS, n_keys, LANES), BF16),
            jax.ShapeDtypeStruct((batch, N_HEADS, n_keys // K_CHUNK, LANES, K_CHUNK), BF16),
            row_out, row_out, row_out, row_out,
        ],
        compiler_params=_params(1),
        name="proj",
    )(h, mods, g, w_in, *tables)


def _attn_kernel(lam_ref, g_ref, qt_ref, k_ref, vt_ref, o_ref, s0_ref, s1_ref, *, n_qblocks, n_chunks):
    tq = Q_BLOCK
    two_tq = 2 * tq
    s_refs = (s0_ref, s1_ref)

    lv = lam_ref[...]
    lam = (jnp.exp(jnp.sum(lv[0:1] * lv[1:2], axis=-1, keepdims=True))
           - jnp.exp(jnp.sum(lv[2:3] * lv[3:4], axis=-1, keepdims=True)) + LAM_INIT)

    def load_qq(i):
        qt = qt_ref[0, 0, i]
        feat = lax.broadcasted_iota(jnp.int32, qt.shape, 0)
        zero = jnp.zeros_like(qt)
        return jnp.concatenate(
            [jnp.where(feat < HEAD_DIM, qt, zero), jnp.where(feat >= HEAD_DIM, qt, zero)], axis=1)

    def fold(x, op):
        return op(x.reshape(K_CHUNK // SUBLANES, SUBLANES, two_tq), axis=0)

    def stage_a(j, s_ref, qq, mx):
        s = _dot(k_ref[0, 0, j * K_CHUNK:(j + 1) * K_CHUNK, :], qq)
        s_ref[j] = s
        return jnp.maximum(mx, fold(s, jnp.max))

    def stage_b(j, s_ref, m, ls, acc):
        e = jnp.exp2(s_ref[j] - m)
        ls = ls + fold(e, jnp.sum)
        acc = acc + _dot(vt_ref[0, 0, j], e.astype(BF16))
        return ls, acc

    mx0 = jnp.full((SUBLANES, two_tq), -jnp.inf, F32)
    ls0 = jnp.zeros((SUBLANES, two_tq), F32)
    acc0 = jnp.zeros((LANES, two_tq), F32)

    def finalize(i, ls, acc):
        l = jnp.sum(ls, axis=0, keepdims=True)
        o = acc[:, :tq] * (1.0 / l[:, :tq]) - acc[:, tq:] * (lam / l[:, tq:])
        on = o * lax.rsqrt(jnp.mean(o * o, axis=0, keepdims=True) + EPS) * g_ref[...] * (1.0 - LAM_INIT)
        off = pl.multiple_of(i * tq, tq)
        o_ref[0, 0, pl.ds(off, tq), :] = on.T.astype(BF16)


    def run_a(i, parity):
        qq = load_qq(i)
        mx = mx0
        for j in range(n_chunks):
            mx = stage_a(j, s_refs[parity], qq, mx)
        return jnp.max(mx, axis=0, keepdims=True)

    def run_b(parity, m):
        ls, acc = ls0, acc0
        for j in range(n_chunks):
            ls, acc = stage_b(j, s_refs[parity], m, ls, acc)
        return ls, acc

    def run_ab(i, parity, m_prev):
        qq = load_qq(i)
        mx, ls, acc = mx0, ls0, acc0
        for j in range(n_chunks):
            mx = stage_a(j, s_refs[parity], qq, mx)
            ls, acc = stage_b(j, s_refs[1 - parity], m_prev, ls, acc)
        return jnp.max(mx, axis=0, keepdims=True), ls, acc

    def pair(p, carry):
        m, ls, acc = carry
        i = 2 * p + 2
        finalize(i - 2, ls, acc)
        m, ls, acc = run_ab(i, 0, m)
        finalize(i - 1, ls, acc)
        return run_ab(i + 1, 1, m)

    assert n_qblocks % 2 == 0
    m = run_a(0, 0)
    carry = run_ab(1, 1, m)
    m, ls, acc = lax.fori_loop(0, n_qblocks // 2 - 1, pair, carry)
    finalize(n_qblocks - 2, ls, acc)
    finalize(n_qblocks - 1, *run_b(1, m))


def _attn(lam_vecs, subln_g, qt, k_all, vt_all):
    batch, heads, n_qblocks = qt.shape[:3]
    n_keys = k_all.shape[2]
    n_chunks = n_keys // K_CHUNK
    seq = n_qblocks * Q_BLOCK
    return pl.pallas_call(
        functools.partial(_attn_kernel, n_qblocks=n_qblocks, n_chunks=n_chunks),
        grid=(batch, heads),
        in_specs=[
            _const_spec(lam_vecs.shape),
            _const_spec(subln_g.shape),
            pl.BlockSpec((1, 1, n_qblocks, LANES, Q_BLOCK), lambda b, h: (b, h, 0, 0, 0)),
            pl.BlockSpec((1, 1, n_keys, LANES), lambda b, h: (b, h, 0, 0)),
            pl.BlockSpec((1, 1, n_chunks, LANES, K_CHUNK), lambda b, h: (b, h, 0, 0, 0)),
        ],
        out_specs=pl.BlockSpec((1, 1, seq, LANES), lambda b, h: (b, h, 0, 0)),
        out_shape=jax.ShapeDtypeStruct((batch, heads, seq, LANES), BF16),
        scratch_shapes=[pltpu.VMEM((n_chunks, K_CHUNK, 2 * Q_BLOCK), F32)] * 2,
        compiler_params=_params(2),
        name="attn",
    )(lam_vecs, subln_g, qt, k_all, vt_all)


def _mixout_kernel(h_ref, mod_ref, o_ref, z_ref, zp_ref, zn_ref, gb_ref, sga_ref, sgb_ref,
                   cw_ref, wpa_ref, wpb_ref, wo_ref, post_ref, out_ref, *, blocks_per_seq):
    tm = h_ref.shape[0]
    i = pl.program_id(0)

    z = z_ref[...].astype(F32)
    pos = i % blocks_per_seq
    zp = zp_ref[...].astype(F32)
    zn = zn_ref[...].astype(F32)
    prev_row = jnp.where(pos == 0, 0.0, zp[zp.shape[0] - 1:, :])
    next_row = jnp.where(pos == blocks_per_seq - 1, 0.0, zn[0:1, :])
    rows = lax.broadcasted_iota(jnp.int32, z.shape, 0)
    z_prev = jnp.where(rows == 0, prev_row, pltpu.roll(z, 1, axis=0))
    z_next = jnp.where(rows == tm - 1, next_row, pltpu.roll(z, tm - 1, axis=0))
    conv = z_prev * cw_ref[0:1, :] + z * cw_ref[1:2, :] + z_next * cw_ref[2:3, :]
    yb = (gb_ref[...].astype(F32) * conv).astype(BF16)

    half = tm // 2
    for r in range(2):
        sl = slice(r * half, (r + 1) * half)
        o = jnp.concatenate([o_ref[0, h, sl, :] for h in range(N_HEADS)], axis=1)
        y_att = _dot(o, wpa_ref[...])
        y_conv = _dot(yb[sl], wpb_ref[...])
        merged = sga_ref[sl, :].astype(F32) * y_att + sgb_ref[sl, :].astype(F32) * y_conv
        y = _dot(merged.astype(BF16), wo_ref[...])
        out_ref[sl, :] = h_ref[sl, :] + mod_ref[0, 5:6, :] * _rms(y, post_ref[...])


def _mixout(h, mods, o, z, gb, sga, sgb, conv_w, wpa, wpb, wo, post_g, *, seq):
    rows, d = h.shape
    tm = ROW_BLOCK
    bpb = seq // tm
    halo = 16
    hpb = tm // halo
    n_halo = rows // halo
    row_spec = pl.BlockSpec((tm, d), lambda i: (i, 0))
    return pl.pallas_call(
        functools.partial(_mixout_kernel, blocks_per_seq=bpb),
        grid=(rows // tm,),
        in_specs=[
            row_spec,
            pl.BlockSpec((1, N_MOD, d), lambda i: (i // bpb, 0, 0)),
            pl.BlockSpec((1, N_HEADS, tm, LANES), lambda i: (i // bpb, 0, i % bpb, 0)),
            row_spec,
            pl.BlockSpec((halo, d), lambda i: (jnp.maximum(i * hpb - 1, 0), 0)),
            pl.BlockSpec((halo, d), lambda i: (jnp.minimum((i + 1) * hpb, n_halo - 1), 0)),
            row_spec, row_spec, row_spec,
            _const_spec(conv_w.shape),
            _const_spec(wpa.shape), _const_spec(wpb.shape), _const_spec(wo.shape),
            _const_spec((1, d)),
        ],
        out_specs=row_spec,
        out_shape=jax.ShapeDtypeStruct((rows, d), F32),
        compiler_params=_params(1),
        name="mixout",
    )(h, mods, o, z, z, z, gb, sga, sgb, conv_w, wpa, wpb, wo, post_g)


def kernel(x, c, ctx, c_ctx, w_mod, b_mod, ffn1_pre_g, ffn1_post_g, ffn1_w_gate, ffn1_w_up, ffn1_w_down, mix_pre_g, mix_post_g, w_in, lam_q1, lam_k1, lam_q2, lam_k2, attn_subln_g, conv_w, w_attn_proj, w_conv_proj, w_out, ffn2_pre_g, ffn2_post_g, ffn2_w_gate, ffn2_w_up, ffn2_w_down):
    batch, seq, d = x.shape
    ctx_len = ctx.shape[1]
    assert w_mod.shape[0] == 1, "single-layer stack only"
    assert batch + 1 <= MOD_ROWS and seq % ROW_BLOCK == 0 and seq % GRID_W == 0

    cond = jnp.zeros((MOD_ROWS, d), F32).at[:batch].set(c).at[batch].set(c_ctx)
    mods = _adaln(cond, w_mod[0], b_mod[0]).reshape(MOD_ROWS, N_MOD, d)

    def row(v):
        return v.reshape(1, -1)

    ffn1_w = [w[0].astype(BF16) for w in (ffn1_w_gate, ffn1_w_up, ffn1_w_down)]
    ffn2_w = [w[0].astype(BF16) for w in (ffn2_w_gate, ffn2_w_up, ffn2_w_down)]
    w_in_b = w_in[0].astype(BF16)

    h = _ffn(x.reshape(batch * seq, d), mods, row(ffn1_pre_g), row(ffn1_post_g), *ffn1_w,
             mod_base=0, mod_row0=0, rows_per_mod_row=seq)
    hc = _ffn(ctx.reshape(batch * ctx_len, d), mods, row(ffn1_pre_g), row(ffn1_post_g), *ffn1_w,
              mod_base=0, mod_row0=batch, rows_per_mod_row=batch * ctx_len)

    qt, k_all, vt_all, z, gb, sga, sgb = _proj(h, mods, row(mix_pre_g), w_in_b,
                                               _rope_tables(seq // GRID_W, ROW_BLOCK),
                                               batch=batch, seq=seq, n_keys=seq + ctx_len)
    k_all, vt_all = _ctx_kv(hc, mods, row(mix_pre_g), w_in_b, k_all, vt_all,
                            batch=batch, ctx_len=ctx_len, mod_row=batch)
    lam_vecs = jnp.concatenate([lam_q1, lam_k1, lam_q2, lam_k2], axis=0)
    o = _attn(lam_vecs, attn_subln_g.reshape(V_DIM, 1), qt, k_all, vt_all)
    h = _mixout(h, mods, o, z, gb, sga, sgb, conv_w[0], w_attn_proj[0].astype(BF16),
                w_conv_proj[0].astype(BF16), w_out[0].astype(BF16), row(mix_post_g), seq=seq)
    h = _ffn(h, mods, row(ffn2_pre_g), row(ffn2_post_g), *ffn2_w,
             mod_base=6, mod_row0=0, rows_per_mod_row=seq)
    return h.reshape(batch, seq, d)
```

```python
import functools
import math

import jax
import jax.numpy as jnp
from jax import lax
from jax.experimental import pallas as pl
from jax.experimental.pallas import tpu as pltpu

F32 = jnp.float32
BF16 = jnp.bfloat16

GRID_W = 64
N_HEADS = 8
HEAD_DIM = 64
V_DIM = 2 * HEAD_DIM
CONV_K = 3
N_MOD = 9
ROPE_BASE = 10000.0
AXIS_DIM = HEAD_DIM // 2
N_FREQ = AXIS_DIM // 2
EPS = 1e-6
LAM_INIT = 0.8 - 0.6 * math.exp(-0.3 * 0)
Q_SCALE = (HEAD_DIM ** -0.5) * math.log2(math.e)

LANES = 128
SUBLANES = 8
MOD_ROWS = 8
VMEM_LIMIT = 56 * 1024 * 1024

ROW_BLOCK = 512
FFN_ROW_BLOCK = 1024
FF_CHUNK = 256
Q_BLOCK = 128
K_CHUNK = 256
K_GROUP = 3
K_STEP = K_GROUP * K_CHUNK
QB_UNROLL = 4


def _dot(a, b):
    return jnp.dot(a, b, preferred_element_type=F32)


def _const_spec(shape):
    return pl.BlockSpec(shape, lambda *_: (0,) * len(shape), pipeline_mode=pl.Buffered(1))


def _rms(x, g):
    return x * lax.rsqrt(jnp.mean(x * x, axis=-1, keepdims=True) + EPS) * g


def _params(n_grid):
    return pltpu.CompilerParams(
        dimension_semantics=("arbitrary",) * n_grid, vmem_limit_bytes=VMEM_LIMIT)


def _adaln_kernel(cond_ref, w_ref, b_ref, o_ref):
    x = cond_ref[...]
    a = x * jax.nn.sigmoid(x)
    a_hi = a.astype(BF16)
    a_lo = (a - a_hi.astype(F32)).astype(BF16)
    w = w_ref[...]
    w_hi = w.astype(BF16)
    w_lo = (w - w_hi.astype(F32)).astype(BF16)
    o_ref[...] = _dot(a_hi, w_hi) + _dot(a_lo, w_hi) + _dot(a_hi, w_lo) + b_ref[...]


def _adaln(cond, w_mod, b_mod):
    d, n = w_mod.shape
    bn = n // 8
    return pl.pallas_call(
        _adaln_kernel,
        grid=(n // bn,),
        in_specs=[
            _const_spec((MOD_ROWS, d)),
            pl.BlockSpec((d, bn), lambda j: (0, j)),
            pl.BlockSpec((1, bn), lambda j: (0, j)),
        ],
        out_specs=pl.BlockSpec((MOD_ROWS, bn), lambda j: (0, j)),
        out_shape=jax.ShapeDtypeStruct((MOD_ROWS, n), F32),
        compiler_params=_params(1),
        name="adaln",
    )(cond, w_mod, b_mod.reshape(1, n))


def _ffn_kernel(x_ref, mod_ref, pre_ref, post_ref, wg_ref, wu_ref, wd_ref, o_ref, a_ref, *, mod_base):
    x = x_ref[...]
    shift = mod_ref[0, mod_base:mod_base + 1, :]
    scale = mod_ref[0, mod_base + 1:mod_base + 2, :]
    gate = mod_ref[0, mod_base + 2:mod_base + 3, :]
    xm = (_rms(x, pre_ref[...]) * (1.0 + scale) + shift).astype(BF16)
    d_ff = wg_ref.shape[1]
    for c in range(d_ff // FF_CHUNK):
        sl = slice(c * FF_CHUNK, (c + 1) * FF_CHUNK)
        g = _dot(xm, wg_ref[:, sl])
        u = _dot(xm, wu_ref[:, sl])
        a_ref[:, sl] = (g * jax.nn.sigmoid(g) * u).astype(BF16)
    y = _dot(a_ref[...], wd_ref[...])
    o_ref[...] = x + 0.5 * gate * _rms(y, post_ref[...])


def _ffn(x, mods, pre_g, post_g, wg, wu, wd, *, mod_base, mod_row0, rows_per_mod_row):
    rows, d = x.shape
    d_ff = wg.shape[1]
    tm = min(FFN_ROW_BLOCK, rows)
    assert rows_per_mod_row % tm == 0
    bpm = rows_per_mod_row // tm
    return pl.pallas_call(
        functools.partial(_ffn_kernel, mod_base=mod_base),
        grid=(rows // tm,),
        in_specs=[
            pl.BlockSpec((tm, d), lambda i: (i, 0)),
            pl.BlockSpec((1, N_MOD, d), lambda i: (mod_row0 + i // bpm, 0, 0)),
            _const_spec((1, d)),
            _const_spec((1, d)),
            _const_spec((d, d_ff)),
            _const_spec((d, d_ff)),
            _const_spec((d_ff, d)),
        ],
        out_specs=pl.BlockSpec((tm, d), lambda i: (i, 0)),
        out_shape=jax.ShapeDtypeStruct((rows, d), F32),
        scratch_shapes=[pltpu.VMEM((tm, d_ff), BF16)],
        compiler_params=_params(1),
        name="ffn",
    )(x, mods, pre_g, post_g, wg, wu, wd)


def _ctx_kv_kernel(x_ref, mod_ref, g_ref, wk_ref, wv_ref, k_in_ref, vt_in_ref, k_ref, vt_ref):
    del k_in_ref, vt_in_ref
    x = x_ref[...]
    shift = mod_ref[0, 3:4, :]
    scale = mod_ref[0, 4:5, :]
    xm = (_rms(x, g_ref[...]) * (1.0 + scale) + shift).astype(BF16)
    k = _dot(xm, wk_ref[...])
    v = _dot(xm, wv_ref[...])
    for h in range(N_HEADS):
        sl = slice(h * LANES, (h + 1) * LANES)
        k_ref[0, h] = k[:, sl].astype(BF16)
        vt_ref[0, h, 0] = v[:, sl].T.astype(BF16)


def _ctx_kv(hc, mods, g, w_in, k_all, vt_all, *, batch, ctx_len, mod_row):
    d = hc.shape[1]
    assert ctx_len == K_CHUNK
    last = vt_all.shape[2] - 1
    return pl.pallas_call(
        _ctx_kv_kernel,
        grid=(batch,),
        in_specs=[
            pl.BlockSpec((ctx_len, d), lambda b: (b, 0)),
            pl.BlockSpec((1, N_MOD, d), lambda b: (mod_row, 0, 0)),
            _const_spec((1, d)),
            pl.BlockSpec((d, d), lambda b: (0, 1), pipeline_mode=pl.Buffered(1)),
            pl.BlockSpec((d, d), lambda b: (0, 2), pipeline_mode=pl.Buffered(1)),
            pl.BlockSpec(memory_space=pl.ANY),
            pl.BlockSpec(memory_space=pl.ANY),
        ],
        out_specs=[
            pl.BlockSpec((1, N_HEADS, ctx_len, LANES), lambda b: (b, 0, last, 0)),
            pl.BlockSpec((1, N_HEADS, 1, LANES, ctx_len), lambda b: (b, 0, last, 0, 0)),
        ],
        out_shape=[
            jax.ShapeDtypeStruct(k_all.shape, BF16),
            jax.ShapeDtypeStruct(vt_all.shape, BF16),
        ],
        input_output_aliases={5: 0, 6: 1},
        compiler_params=_params(1),
        name="ctx_kv",
    )(hc, mods, g, w_in, w_in, k_all, vt_all)


def _rope_tables(n_rows, tm):
    inv_freq = ROPE_BASE ** (-2.0 * jnp.arange(N_FREQ, dtype=F32) / AXIS_DIM)
    lane = jnp.arange(LANES)
    row_lane = (lane % HEAD_DIM) < AXIS_DIM
    first_half = (lane % AXIS_DIM) < N_FREQ

    def factors(pos, keep):
        ang = jnp.tile(pos.astype(F32)[:, None] * inv_freq, (1, LANES // N_FREQ))
        cos, sin = jnp.cos(ang), jnp.sin(ang)
        tabs = (cos, jnp.where(first_half, -sin, 0.0), jnp.where(first_half, 0.0, sin))
        return [jnp.where(keep, t, 0.0) for t in tabs]

    return (factors(jnp.arange(n_rows), row_lane)
            + factors(jnp.arange(tm) % GRID_W, jnp.logical_not(row_lane)))


def _proj_kernel(x_ref, mod_ref, g_ref, w_ref, rcos_ref, rsup_ref, rsdn_ref, ccos_ref, csup_ref, csdn_ref,
                 qt_ref, k_ref, vt_ref, z_ref, gb_ref, sga_ref, sgb_ref):
    tm, d = x_ref.shape
    x = x_ref[...]
    shift = mod_ref[0, 3:4, :]
    scale = mod_ref[0, 4:5, :]
    xm = (_rms(x, g_ref[...]) * (1.0 + scale) + shift).astype(BF16)

    def table(row_ref, col_ref):
        rt = row_ref[...]
        rows = [jnp.broadcast_to(rt[r:r + 1, :], (GRID_W, LANES)) for r in range(tm // GRID_W)]
        return jnp.concatenate(rows, axis=0) + col_ref[...]

    cos, sup, sdn = table(rcos_ref, ccos_ref), table(rsup_ref, csup_ref), table(rsdn_ref, csdn_ref)

    def group(j):
        return _dot(xm, w_ref[:, j * d:(j + 1) * d])

    def rope(xh):
        return (xh * cos + pltpu.roll(xh, LANES - N_FREQ, axis=1) * sup
                + pltpu.roll(xh, N_FREQ, axis=1) * sdn)

    uq = group(0)
    for h in range(N_HEADS):
        qh = rope(uq[:, h * LANES:(h + 1) * LANES]) * Q_SCALE
        for r in range(tm // Q_BLOCK):
            qt_ref[0, h, r] = qh[r * Q_BLOCK:(r + 1) * Q_BLOCK, :].T.astype(BF16)
    uk = group(1)
    for h in range(N_HEADS):
        k_ref[0, h] = rope(uk[:, h * LANES:(h + 1) * LANES]).astype(BF16)
    uv = group(2)
    for h in range(N_HEADS):
        vh = uv[:, h * LANES:(h + 1) * LANES]
        for r in range(tm // K_CHUNK):
            vt_ref[0, h, r] = vh[r * K_CHUNK:(r + 1) * K_CHUNK, :].T.astype(BF16)
    gb_ref[...] = group(3).astype(BF16)
    z_ref[...] = (group(4) * group(5)).astype(BF16)
    sga_ref[...] = jax.nn.sigmoid(group(6)).astype(BF16)
    sgb_ref[...] = jax.nn.sigmoid(group(7)).astype(BF16)


def _proj(h, mods, g, w_in, tables, *, batch, seq, n_keys):
    rows, d = h.shape
    tm = ROW_BLOCK
    bpb = seq // tm
    qpb, cpb = tm // Q_BLOCK, tm // K_CHUNK
    row_spec = pl.BlockSpec((tm, d), lambda i: (i, 0))
    row_out = jax.ShapeDtypeStruct((rows, d), BF16)
    row_tab = pl.BlockSpec((tm // GRID_W, LANES), lambda i: (i % bpb, 0))
    col_tab = _const_spec((tm, LANES))
    return pl.pallas_call(
        _proj_kernel,
        grid=(rows // tm,),
        in_specs=[
            row_spec,
            pl.BlockSpec((1, N_MOD, d), lambda i: (i // bpb, 0, 0)),
            _const_spec((1, d)),
            _const_spec(w_in.shape),
            row_tab, row_tab, row_tab, col_tab, col_tab, col_tab,
        ],
        out_specs=[
            pl.BlockSpec((1, N_HEADS, qpb, LANES, Q_BLOCK), lambda i: (i // bpb, 0, i % bpb, 0, 0)),
            pl.BlockSpec((1, N_HEADS, tm, LANES), lambda i: (i // bpb, 0, i % bpb, 0)),
            pl.BlockSpec((1, N_HEADS, cpb, LANES, K_CHUNK), lambda i: (i // bpb, 0, i % bpb, 0, 0)),
            row_spec, row_spec, row_spec, row_spec,
        ],
        out_shape=[
            jax.ShapeDtypeStruct((batch, N_HEADS, seq // Q_BLOCK, LANES, Q_BLOCK), BF16),
            jax.ShapeDtypeStruct((batch, N_HEADS, n_keys, LANES), BF16),
            jax.ShapeDtypeStruct((batch, N_HEADS, n_keys // K_CHUNK, LANES, K_CHUNK), BF16),
            row_out, row_out, row_out, row_out,
        ],
        compiler_params=_params(1),
        name="proj",
    )(h, mods, g, w_in, *tables)


def _attn_kernel(lam_ref, g_ref, qt_ref, k_ref, vt_ref, o_ref, s0_ref, s1_ref, *, n_qblocks, n_chunks):
    tq = Q_BLOCK
    two_tq = 2 * tq
    s_refs = (s0_ref, s1_ref)

    lv = lam_ref[...]
    lam = (jnp.exp(jnp.sum(lv[0:1] * lv[1:2], axis=-1, keepdims=True))
           - jnp.exp(jnp.sum(lv[2:3] * lv[3:4], axis=-1, keepdims=True)) + LAM_INIT)

    def load_qq(i):
        qt = qt_ref[0, 0, i]
        feat = lax.broadcasted_iota(jnp.int32, qt.shape, 0)
        zero = jnp.zeros_like(qt)
        return jnp.concatenate(
            [jnp.where(feat < HEAD_DIM, qt, zero), jnp.where(feat >= HEAD_DIM, qt, zero)], axis=1)

    def fold(x, op):
        return op(x.reshape(K_STEP // SUBLANES, SUBLANES, two_tq), axis=0)

    def stage_a(j, s_ref, qq, mx):
        s = _dot(k_ref[0, 0, j * K_STEP:(j + 1) * K_STEP, :], qq)
        s_ref[j] = s
        return jnp.maximum(mx, fold(s, jnp.max))

    def stage_b(j, s_ref, m, ls, acc):
        e = jnp.exp2(s_ref[j] - m)
        ls = ls + fold(e, jnp.sum)
        vt = jnp.concatenate([vt_ref[0, 0, K_GROUP * j + t] for t in range(K_GROUP)], axis=1)
        acc = acc + _dot(vt, e.astype(BF16))
        return ls, acc

    mx0 = jnp.full((SUBLANES, two_tq), -jnp.inf, F32)
    ls0 = jnp.zeros((SUBLANES, two_tq), F32)
    acc0 = jnp.zeros((LANES, two_tq), F32)

    def finalize(i, ls, acc):
        l = jnp.sum(ls, axis=0, keepdims=True)
        o = acc[:, :tq] * (1.0 / l[:, :tq]) - acc[:, tq:] * (lam / l[:, tq:])
        on = o * lax.rsqrt(jnp.mean(o * o, axis=0, keepdims=True) + EPS) * g_ref[...] * (1.0 - LAM_INIT)
        off = pl.multiple_of(i * tq, tq)
        o_ref[0, 0, pl.ds(off, tq), :] = on.T.astype(BF16)


    def run_a(i, parity):
        qq = load_qq(i)
        mx = mx0
        for j in range(n_chunks):
            mx = stage_a(j, s_refs[parity], qq, mx)
        return jnp.max(mx, axis=0, keepdims=True)

    def run_b(parity, m):
        ls, acc = ls0, acc0
        for j in range(n_chunks):
            ls, acc = stage_b(j, s_refs[parity], m, ls, acc)
        return ls, acc

    def run_ab(i, parity, m_prev):
        qq = load_qq(i)
        mx, ls, acc = mx0, ls0, acc0
        for j in range(n_chunks):
            mx = stage_a(j, s_refs[parity], qq, mx)
            ls, acc = stage_b(j, s_refs[1 - parity], m_prev, ls, acc)
        return jnp.max(mx, axis=0, keepdims=True), ls, acc

    def blocks(first, count, carry):
        m, ls, acc = carry
        for t in range(count):
            finalize(first + t - 2, ls, acc)
            m, ls, acc = run_ab(first + t, t % 2, m)
        return m, ls, acc

    assert n_qblocks % QB_UNROLL == 0 and QB_UNROLL % 2 == 0 and n_qblocks >= 2 * QB_UNROLL
    m = run_a(0, 0)
    carry = run_ab(1, 1, m)
    carry = blocks(2, QB_UNROLL - 2, carry)
    m, ls, acc = lax.fori_loop(1, n_qblocks // QB_UNROLL,
                               lambda p, c: blocks(QB_UNROLL * p, QB_UNROLL, c), carry)
    finalize(n_qblocks - 2, ls, acc)
    finalize(n_qblocks - 1, *run_b(1, m))


def _attn(lam_vecs, subln_g, qt, k_all, vt_all):
    batch, heads, n_qblocks = qt.shape[:3]
    n_keys = k_all.shape[2]
    n_chunks = n_keys // K_CHUNK
    assert n_chunks % K_GROUP == 0
    seq = n_qblocks * Q_BLOCK
    return pl.pallas_call(
        functools.partial(_attn_kernel, n_qblocks=n_qblocks, n_chunks=n_chunks // K_GROUP),
        grid=(batch, heads),
        in_specs=[
            _const_spec(lam_vecs.shape),
            _const_spec(subln_g.shape),
            pl.BlockSpec((1, 1, n_qblocks, LANES, Q_BLOCK), lambda b, h: (b, h, 0, 0, 0)),
            pl.BlockSpec((1, 1, n_keys, LANES), lambda b, h: (b, h, 0, 0)),
            pl.BlockSpec((1, 1, n_chunks, LANES, K_CHUNK), lambda b, h: (b, h, 0, 0, 0)),
        ],
        out_specs=pl.BlockSpec((1, 1, seq, LANES), lambda b, h: (b, h, 0, 0)),
        out_shape=jax.ShapeDtypeStruct((batch, heads, seq, LANES), BF16),
        scratch_shapes=[pltpu.VMEM((n_chunks // K_GROUP, K_STEP, 2 * Q_BLOCK), F32)] * 2,
        compiler_params=_params(2),
        name="attn",
    )(lam_vecs, subln_g, qt, k_all, vt_all)


def _mixout_kernel(h_ref, mod_ref, o_ref, z_ref, zp_ref, zn_ref, gb_ref, sga_ref, sgb_ref,
                   cw_ref, wpa_ref, wpb_ref, wo_ref, post_ref, out_ref, *, blocks_per_seq):
    tm = h_ref.shape[0]
    i = pl.program_id(0)

    z = z_ref[...].astype(F32)
    pos = i % blocks_per_seq
    zp = zp_ref[...].astype(F32)
    zn = zn_ref[...].astype(F32)
    prev_row = jnp.where(pos == 0, 0.0, zp[zp.shape[0] - 1:, :])
    next_row = jnp.where(pos == blocks_per_seq - 1, 0.0, zn[0:1, :])
    rows = lax.broadcasted_iota(jnp.int32, z.shape, 0)
    z_prev = jnp.where(rows == 0, prev_row, pltpu.roll(z, 1, axis=0))
    z_next = jnp.where(rows == tm - 1, next_row, pltpu.roll(z, tm - 1, axis=0))
    conv = z_prev * cw_ref[0:1, :] + z * cw_ref[1:2, :] + z_next * cw_ref[2:3, :]
    yb = (gb_ref[...].astype(F32) * conv).astype(BF16)

    half = tm // 2
    for r in range(2):
        sl = slice(r * half, (r + 1) * half)
        o = jnp.concatenate([o_ref[0, h, sl, :] for h in range(N_HEADS)], axis=1)
        y_att = _dot(o, wpa_ref[...])
        y_conv = _dot(yb[sl], wpb_ref[...])
        merged = sga_ref[sl, :].astype(F32) * y_att + sgb_ref[sl, :].astype(F32) * y_conv
        y = _dot(merged.astype(BF16), wo_ref[...])
        out_ref[sl, :] = h_ref[sl, :] + mod_ref[0, 5:6, :] * _rms(y, post_ref[...])


def _mixout(h, mods, o, z, gb, sga, sgb, conv_w, wpa, wpb, wo, post_g, *, seq):
    rows, d = h.shape
    tm = ROW_BLOCK
    bpb = seq // tm
    halo = 16
    hpb = tm // halo
    n_halo = rows // halo
    row_spec = pl.BlockSpec((tm, d), lambda i: (i, 0))
    return pl.pallas_call(
        functools.partial(_mixout_kernel, blocks_per_seq=bpb),
        grid=(rows // tm,),
        in_specs=[
            row_spec,
            pl.BlockSpec((1, N_MOD, d), lambda i: (i // bpb, 0, 0)),
            pl.BlockSpec((1, N_HEADS, tm, LANES), lambda i: (i // bpb, 0, i % bpb, 0)),
            row_spec,
            pl.BlockSpec((halo, d), lambda i: (jnp.maximum(i * hpb - 1, 0), 0)),
            pl.BlockSpec((halo, d), lambda i: (jnp.minimum((i + 1) * hpb, n_halo - 1), 0)),
            row_spec, row_spec, row_spec,
            _const_spec(conv_w.shape),
            _const_spec(wpa.shape), _const_spec(wpb.shape), _const_spec(wo.shape),
            _const_spec((1, d)),
        ],
        out_specs=row_spec,
        out_shape=jax.ShapeDtypeStruct((rows, d), F32),
        compiler_params=_params(1),
        name="mixout",
    )(h, mods, o, z, z, z, gb, sga, sgb, conv_w, wpa, wpb, wo, post_g)


def kernel(x, c, ctx, c_ctx, w_mod, b_mod, ffn1_pre_g, ffn1_post_g, ffn1_w_gate, ffn1_w_up, ffn1_w_down, mix_pre_g, mix_post_g, w_in, lam_q1, lam_k1, lam_q2, lam_k2, attn_subln_g, conv_w, w_attn_proj, w_conv_proj, w_out, ffn2_pre_g, ffn2_post_g, ffn2_w_gate, ffn2_w_up, ffn2_w_down):
    batch, seq, d = x.shape
    ctx_len = ctx.shape[1]
    assert w_mod.shape[0] == 1, "single-layer stack only"
    assert batch + 1 <= MOD_ROWS and seq % ROW_BLOCK == 0 and seq % GRID_W == 0

    cond = jnp.zeros((MOD_ROWS, d), F32).at[:batch].set(c).at[batch].set(c_ctx)
    mods = _adaln(cond, w_mod[0], b_mod[0]).reshape(MOD_ROWS, N_MOD, d)

    def row(v):
        return v.reshape(1, -1)

    ffn1_w = [w[0].astype(BF16) for w in (ffn1_w_gate, ffn1_w_up, ffn1_w_down)]
    ffn2_w = [w[0].astype(BF16) for w in (ffn2_w_gate, ffn2_w_up, ffn2_w_down)]
    w_in_b = w_in[0].astype(BF16)

    h = _ffn(x.reshape(batch * seq, d), mods, row(ffn1_pre_g), row(ffn1_post_g), *ffn1_w,
             mod_base=0, mod_row0=0, rows_per_mod_row=seq)
    hc = _ffn(ctx.reshape(batch * ctx_len, d), mods, row(ffn1_pre_g), row(ffn1_post_g), *ffn1_w,
              mod_base=0, mod_row0=batch, rows_per_mod_row=batch * ctx_len)

    qt, k_all, vt_all, z, gb, sga, sgb = _proj(h, mods, row(mix_pre_g), w_in_b,
                                               _rope_tables(seq // GRID_W, ROW_BLOCK),
                                               batch=batch, seq=seq, n_keys=seq + ctx_len)
    k_all, vt_all = _ctx_kv(hc, mods, row(mix_pre_g), w_in_b, k_all, vt_all,
                            batch=batch, ctx_len=ctx_len, mod_row=batch)
    lam_vecs = jnp.concatenate([lam_q1, lam_k1, lam_q2, lam_k2], axis=0)
    o = _attn(lam_vecs, attn_subln_g.reshape(V_DIM, 1), qt, k_all, vt_all)
    h = _mixout(h, mods, o, z, gb, sga, sgb, conv_w[0], w_attn_proj[0].astype(BF16),
                w_conv_proj[0].astype(BF16), w_out[0].astype(BF16), row(mix_post_g), seq=seq)
    h = _ffn(h, mods, row(ffn2_pre_g), row(ffn2_post_g), *ffn2_w,
             mod_base=6, mod_row0=0, rows_per_mod_row=seq)
    return h.reshape(batch, seq, d)
```

```python
import functools
import math

import jax
import jax.numpy as jnp
from jax import lax
from jax.experimental import pallas as pl
from jax.experimental.pallas import tpu as pltpu

F32 = jnp.float32
BF16 = jnp.bfloat16

GRID_W = 64
N_HEADS = 8
HEAD_DIM = 64
V_DIM = 2 * HEAD_DIM
CONV_K = 3
N_MOD = 9
ROPE_BASE = 10000.0
AXIS_DIM = HEAD_DIM // 2
N_FREQ = AXIS_DIM // 2
EPS = 1e-6
LAM_INIT = 0.8 - 0.6 * math.exp(-0.3 * 0)
Q_SCALE = (HEAD_DIM ** -0.5) * math.log2(math.e)

LANES = 128
SUBLANES = 8
MOD_ROWS = 8
VMEM_LIMIT = 56 * 1024 * 1024

ROW_BLOCK = 512
FFN_ROW_BLOCK = 1024
FF_CHUNK = 256
FF_PART_MIN = 512
Q_BLOCK = 128
K_CHUNK = 256
K_GROUP = 3
K_STEP = K_GROUP * K_CHUNK
QB_UNROLL = 4


def _dot(a, b):
    return jnp.dot(a, b, preferred_element_type=F32)


def _const_spec(shape):
    return pl.BlockSpec(shape, lambda *_: (0,) * len(shape), pipeline_mode=pl.Buffered(1))


def _rms(x, g):
    return x * lax.rsqrt(jnp.mean(x * x, axis=-1, keepdims=True) + EPS) * g


def _params(n_grid):
    return pltpu.CompilerParams(
        dimension_semantics=("arbitrary",) * n_grid, vmem_limit_bytes=VMEM_LIMIT)


def _cast_specs(mats, steps, step_of):
    for w in mats:
        assert w.shape[0] % (steps * 2 * SUBLANES) == 0, (w.shape, steps)

    def specs():
        return [pl.BlockSpec((w.shape[0] // steps, w.shape[1]), lambda *g: (step_of(*g), 0))
                for w in mats]

    return specs(), specs(), [jax.ShapeDtypeStruct(w.shape, BF16) for w in mats]


def _cast_blocks(src_refs, dst_refs):
    for src, dst in zip(src_refs, dst_refs):
        dst[...] = src[...].astype(BF16)


def _adaln_kernel(cond_ref, w_ref, b_ref, o_ref):
    x = cond_ref[...]
    a = x * jax.nn.sigmoid(x)
    a_hi = a.astype(BF16)
    a_lo = (a - a_hi.astype(F32)).astype(BF16)
    w = w_ref[...]
    w_hi = w.astype(BF16)
    w_lo = (w - w_hi.astype(F32)).astype(BF16)
    o_ref[...] = _dot(a_hi, w_hi) + _dot(a_lo, w_hi) + _dot(a_hi, w_lo) + b_ref[...]


def _adaln(cond, w_mod, b_mod):
    d, n = w_mod.shape
    bn = n // 8
    return pl.pallas_call(
        _adaln_kernel,
        grid=(n // bn,),
        in_specs=[
            _const_spec((MOD_ROWS, d)),
            pl.BlockSpec((d, bn), lambda j: (0, j)),
            pl.BlockSpec((1, bn), lambda j: (0, j)),
        ],
        out_specs=pl.BlockSpec((MOD_ROWS, bn), lambda j: (0, j)),
        out_shape=jax.ShapeDtypeStruct((MOD_ROWS, n), F32),
        compiler_params=_params(1),
        name="adaln",
    )(cond, w_mod, b_mod.reshape(1, n))


def _ffn_kernel(x_ref, mod_ref, pre_ref, post_ref, wg_ref, wu_ref, wd_ref, *rest, mod_base, n_cast):
    cast_src, o_ref, cast_dst, a_ref = rest[:n_cast], rest[n_cast], rest[n_cast + 1:-1], rest[-1]
    _cast_blocks(cast_src, cast_dst)
    shift = mod_ref[0, mod_base:mod_base + 1, :]
    scale = mod_ref[0, mod_base + 1:mod_base + 2, :]
    gate = mod_ref[0, mod_base + 2:mod_base + 3, :]
    d_ff = wg_ref.shape[1]
    tm = x_ref.shape[0]
    n_parts = 2 if tm % (2 * FF_PART_MIN) == 0 else 1
    part = tm // n_parts
    for r in range(n_parts):
        rows = slice(r * part, (r + 1) * part)
        x = x_ref[rows, :]
        xm = (_rms(x, pre_ref[...]) * (1.0 + scale) + shift).astype(BF16)
        for c in range(d_ff // FF_CHUNK):
            sl = slice(c * FF_CHUNK, (c + 1) * FF_CHUNK)
            g = _dot(xm, wg_ref[:, sl])
            u = _dot(xm, wu_ref[:, sl])
            a_ref[rows, sl] = (g * jax.nn.sigmoid(g) * u).astype(BF16)
        y = _dot(a_ref[rows, :], wd_ref[...])
        o_ref[rows, :] = x + 0.5 * gate * _rms(y, post_ref[...])


def _ffn(x, mods, pre_g, post_g, wg, wu, wd, *, mod_base, mod_row0, rows_per_mod_row, tm, cast=()):
    rows, d = x.shape
    d_ff = wg.shape[1]
    cast_in, cast_out, cast_shapes = _cast_specs(cast, rows // tm, lambda i: i)
    assert rows_per_mod_row % tm == 0
    bpm = rows_per_mod_row // tm
    return pl.pallas_call(
        functools.partial(_ffn_kernel, mod_base=mod_base, n_cast=len(cast)),
        grid=(rows // tm,),
        in_specs=[
            pl.BlockSpec((tm, d), lambda i: (i, 0)),
            pl.BlockSpec((1, N_MOD, d), lambda i: (mod_row0 + i // bpm, 0, 0)),
            _const_spec((1, d)),
            _const_spec((1, d)),
            _const_spec((d, d_ff)),
            _const_spec((d, d_ff)),
            _const_spec((d_ff, d)),
            *cast_in,
        ],
        out_specs=[pl.BlockSpec((tm, d), lambda i: (i, 0)), *cast_out],
        out_shape=[jax.ShapeDtypeStruct((rows, d), F32), *cast_shapes],
        scratch_shapes=[pltpu.VMEM((tm, d_ff), BF16)],
        compiler_params=_params(1),
        name="ffn",
    )(x, mods, pre_g, post_g, wg, wu, wd, *cast)


def _ctx_kv_kernel(x_ref, mod_ref, g_ref, wk_ref, wv_ref, k_in_ref, vt_in_ref, k_ref, vt_ref):
    del k_in_ref, vt_in_ref
    x = x_ref[...]
    shift = mod_ref[0, 3:4, :]
    scale = mod_ref[0, 4:5, :]
    xm = (_rms(x, g_ref[...]) * (1.0 + scale) + shift).astype(BF16)
    k = _dot(xm, wk_ref[...])
    v = _dot(xm, wv_ref[...])
    for h in range(N_HEADS):
        sl = slice(h * LANES, (h + 1) * LANES)
        k_ref[0, h] = k[:, sl].astype(BF16)
        vt_ref[0, h, 0] = v[:, sl].T.astype(BF16)


def _ctx_kv(hc, mods, g, w_in, k_all, vt_all, *, batch, ctx_len, mod_row):
    d = hc.shape[1]
    assert ctx_len == K_CHUNK
    last = vt_all.shape[2] - 1
    return pl.pallas_call(
        _ctx_kv_kernel,
        grid=(batch,),
        in_specs=[
            pl.BlockSpec((ctx_len, d), lambda b: (b, 0)),
            pl.BlockSpec((1, N_MOD, d), lambda b: (mod_row, 0, 0)),
            _const_spec((1, d)),
            pl.BlockSpec((d, d), lambda b: (0, 1), pipeline_mode=pl.Buffered(1)),
            pl.BlockSpec((d, d), lambda b: (0, 2), pipeline_mode=pl.Buffered(1)),
            pl.BlockSpec(memory_space=pl.ANY),
            pl.BlockSpec(memory_space=pl.ANY),
        ],
        out_specs=[
            pl.BlockSpec((1, N_HEADS, ctx_len, LANES), lambda b: (b, 0, last, 0)),
            pl.BlockSpec((1, N_HEADS, 1, LANES, ctx_len), lambda b: (b, 0, last, 0, 0)),
        ],
        out_shape=[
            jax.ShapeDtypeStruct(k_all.shape, BF16),
            jax.ShapeDtypeStruct(vt_all.shape, BF16),
        ],
        input_output_aliases={5: 0, 6: 1},
        compiler_params=_params(1),
        name="ctx_kv",
    )(hc, mods, g, w_in, w_in, k_all, vt_all)


def _rope_tables(n_rows, tm):
    inv_freq = ROPE_BASE ** (-2.0 * jnp.arange(N_FREQ, dtype=F32) / AXIS_DIM)
    lane = jnp.arange(LANES)
    row_lane = (lane % HEAD_DIM) < AXIS_DIM
    first_half = (lane % AXIS_DIM) < N_FREQ

    def factors(pos, keep):
        ang = jnp.tile(pos.astype(F32)[:, None] * inv_freq, (1, LANES // N_FREQ))
        cos, sin = jnp.cos(ang), jnp.sin(ang)
        tabs = (cos, jnp.where(first_half, -sin, 0.0), jnp.where(first_half, 0.0, sin))
        return [jnp.where(keep, t, 0.0) for t in tabs]

    return (factors(jnp.arange(n_rows), row_lane)
            + factors(jnp.arange(tm) % GRID_W, jnp.logical_not(row_lane)))


def _proj_kernel(x_ref, mod_ref, g_ref, w_ref, rcos_ref, rsup_ref, rsdn_ref, ccos_ref, csup_ref, csdn_ref,
                 qt_ref, k_ref, vt_ref, z_ref, gb_ref, sga_ref, sgb_ref):
    tm, d = x_ref.shape
    x = x_ref[...]
    shift = mod_ref[0, 3:4, :]
    scale = mod_ref[0, 4:5, :]
    xm = (_rms(x, g_ref[...]) * (1.0 + scale) + shift).astype(BF16)

    def table(row_ref, col_ref):
        rt = row_ref[...]
        rows = [jnp.broadcast_to(rt[r:r + 1, :], (GRID_W, LANES)) for r in range(tm // GRID_W)]
        return jnp.concatenate(rows, axis=0) + col_ref[...]

    cos, sup, sdn = table(rcos_ref, ccos_ref), table(rsup_ref, csup_ref), table(rsdn_ref, csdn_ref)

    def group(j):
        return _dot(xm, w_ref[:, j * d:(j + 1) * d])

    def rope(xh):
        return (xh * cos + pltpu.roll(xh, LANES - N_FREQ, axis=1) * sup
                + pltpu.roll(xh, N_FREQ, axis=1) * sdn)

    uq = group(0)
    for h in range(N_HEADS):
        qh = rope(uq[:, h * LANES:(h + 1) * LANES]) * Q_SCALE
        for r in range(tm // Q_BLOCK):
            qt_ref[0, h, r] = qh[r * Q_BLOCK:(r + 1) * Q_BLOCK, :].T.astype(BF16)
    uk = group(1)
    for h in range(N_HEADS):
        k_ref[0, h] = rope(uk[:, h * LANES:(h + 1) * LANES]).astype(BF16)
    uv = group(2)
    for h in range(N_HEADS):
        vh = uv[:, h * LANES:(h + 1) * LANES]
        for r in range(tm // K_CHUNK):
            vt_ref[0, h, r] = vh[r * K_CHUNK:(r + 1) * K_CHUNK, :].T.astype(BF16)
    gb_ref[...] = group(3).astype(BF16)
    z_ref[...] = (group(4) * group(5)).astype(BF16)
    sga_ref[...] = jax.nn.sigmoid(group(6)).astype(BF16)
    sgb_ref[...] = jax.nn.sigmoid(group(7)).astype(BF16)


def _proj(h, mods, g, w_in, tables, *, batch, seq, n_keys):
    rows, d = h.shape
    tm = ROW_BLOCK
    bpb = seq // tm
    qpb, cpb = tm // Q_BLOCK, tm // K_CHUNK
    row_spec = pl.BlockSpec((tm, d), lambda i: (i, 0))
    row_out = jax.ShapeDtypeStruct((rows, d), BF16)
    row_tab = pl.BlockSpec((tm // GRID_W, LANES), lambda i: (i % bpb, 0))
    col_tab = _const_spec((tm, LANES))
    return pl.pallas_call(
        _proj_kernel,
        grid=(rows // tm,),
        in_specs=[
            row_spec,
            pl.BlockSpec((1, N_MOD, d), lambda i: (i // bpb, 0, 0)),
            _const_spec((1, d)),
            _const_spec(w_in.shape),
            row_tab, row_tab, row_tab, col_tab, col_tab, col_tab,
        ],
        out_specs=[
            pl.BlockSpec((1, N_HEADS, qpb, LANES, Q_BLOCK), lambda i: (i // bpb, 0, i % bpb, 0, 0)),
            pl.BlockSpec((1, N_HEADS, tm, LANES), lambda i: (i // bpb, 0, i % bpb, 0)),
            pl.BlockSpec((1, N_HEADS, cpb, LANES, K_CHUNK), lambda i: (i // bpb, 0, i % bpb, 0, 0)),
            row_spec, row_spec, row_spec, row_spec,
        ],
        out_shape=[
            jax.ShapeDtypeStruct((batch, N_HEADS, seq // Q_BLOCK, LANES, Q_BLOCK), BF16),
            jax.ShapeDtypeStruct((batch, N_HEADS, n_keys, LANES), BF16),
            jax.ShapeDtypeStruct((batch, N_HEADS, n_keys // K_CHUNK, LANES, K_CHUNK), BF16),
            row_out, row_out, row_out, row_out,
        ],
        compiler_params=_params(1),
        name="proj",
    )(h, mods, g, w_in, *tables)


def _attn_kernel(lam_ref, g_ref, qt_ref, k_ref, vt_ref, *rest, n_qblocks, n_chunks, n_cast):
    cast_src, o_ref, cast_dst = rest[:n_cast], rest[n_cast], rest[n_cast + 1:-2]
    s0_ref, s1_ref = rest[-2:]
    _cast_blocks(cast_src, cast_dst)
    tq = Q_BLOCK
    two_tq = 2 * tq
    s_refs = (s0_ref, s1_ref)

    lv = lam_ref[...]
    lam = (jnp.exp(jnp.sum(lv[0:1] * lv[1:2], axis=-1, keepdims=True))
           - jnp.exp(jnp.sum(lv[2:3] * lv[3:4], axis=-1, keepdims=True)) + LAM_INIT)

    def load_qq(i):
        qt = qt_ref[0, 0, i]
        feat = lax.broadcasted_iota(jnp.int32, qt.shape, 0)
        zero = jnp.zeros_like(qt)
        return jnp.concatenate(
            [jnp.where(feat < HEAD_DIM, qt, zero), jnp.where(feat >= HEAD_DIM, qt, zero)], axis=1)

    def fold(x, op):
        return op(x.reshape(K_STEP // SUBLANES, SUBLANES, two_tq), axis=0)

    def stage_a(j, s_ref, qq, mx):
        s = _dot(k_ref[0, 0, j * K_STEP:(j + 1) * K_STEP, :], qq)
        s_ref[j] = s
        return jnp.maximum(mx, fold(s, jnp.max))

    def stage_b(j, s_ref, m, ls, acc):
        e = jnp.exp2(s_ref[j] - m)
        ls = ls + fold(e, jnp.sum)
        vt = jnp.concatenate([vt_ref[0, 0, K_GROUP * j + t] for t in range(K_GROUP)], axis=1)
        acc = acc + _dot(vt, e.astype(BF16))
        return ls, acc

    mx0 = jnp.full((SUBLANES, two_tq), -jnp.inf, F32)
    ls0 = jnp.zeros((SUBLANES, two_tq), F32)
    acc0 = jnp.zeros((LANES, two_tq), F32)

    def finalize(i, ls, acc):
        l = jnp.sum(ls, axis=0, keepdims=True)
        o = acc[:, :tq] * (1.0 / l[:, :tq]) - acc[:, tq:] * (lam / l[:, tq:])
        on = o * lax.rsqrt(jnp.mean(o * o, axis=0, keepdims=True) + EPS) * g_ref[...] * (1.0 - LAM_INIT)
        off = pl.multiple_of(i * tq, tq)
        o_ref[0, 0, pl.ds(off, tq), :] = on.T.astype(BF16)


    def run_a(i, parity):
        qq = load_qq(i)
        mx = mx0
        for j in range(n_chunks):
            mx = stage_a(j, s_refs[parity], qq, mx)
        return jnp.max(mx, axis=0, keepdims=True)

    def run_b(parity, m):
        ls, acc = ls0, acc0
        for j in range(n_chunks):
            ls, acc = stage_b(j, s_refs[parity], m, ls, acc)
        return ls, acc

    def run_ab(i, parity, m_prev):
        qq = load_qq(i)
        mx, ls, acc = mx0, ls0, acc0
        for j in range(n_chunks):
            mx = stage_a(j, s_refs[parity], qq, mx)
            ls, acc = stage_b(j, s_refs[1 - parity], m_prev, ls, acc)
        return jnp.max(mx, axis=0, keepdims=True), ls, acc

    def blocks(first, count, carry):
        m, ls, acc = carry
        for t in range(count):
            finalize(first + t - 2, ls, acc)
            m, ls, acc = run_ab(first + t, t % 2, m)
        return m, ls, acc

    assert n_qblocks % QB_UNROLL == 0 and QB_UNROLL % 2 == 0 and n_qblocks >= 2 * QB_UNROLL
    m = run_a(0, 0)
    carry = run_ab(1, 1, m)
    carry = blocks(2, QB_UNROLL - 2, carry)
    m, ls, acc = lax.fori_loop(1, n_qblocks // QB_UNROLL,
                               lambda p, c: blocks(QB_UNROLL * p, QB_UNROLL, c), carry)
    finalize(n_qblocks - 2, ls, acc)
    finalize(n_qblocks - 1, *run_b(1, m))


def _attn(lam_vecs, subln_g, qt, k_all, vt_all, cast=()):
    batch, heads, n_qblocks = qt.shape[:3]
    cast_in, cast_out, cast_shapes = _cast_specs(cast, batch * heads, lambda b, h: b * heads + h)
    n_keys = k_all.shape[2]
    n_chunks = n_keys // K_CHUNK
    assert n_chunks % K_GROUP == 0
    seq = n_qblocks * Q_BLOCK
    return pl.pallas_call(
        functools.partial(_attn_kernel, n_qblocks=n_qblocks, n_chunks=n_chunks // K_GROUP,
                          n_cast=len(cast)),
        grid=(batch, heads),
        in_specs=[
            _const_spec(lam_vecs.shape),
            _const_spec(subln_g.shape),
            pl.BlockSpec((1, 1, n_qblocks, LANES, Q_BLOCK), lambda b, h: (b, h, 0, 0, 0)),
            pl.BlockSpec((1, 1, n_keys, LANES), lambda b, h: (b, h, 0, 0)),
            pl.BlockSpec((1, 1, n_chunks, LANES, K_CHUNK), lambda b, h: (b, h, 0, 0, 0)),
            *cast_in,
        ],
        out_specs=[pl.BlockSpec((1, 1, seq, LANES), lambda b, h: (b, h, 0, 0)), *cast_out],
        out_shape=[jax.ShapeDtypeStruct((batch, heads, seq, LANES), BF16), *cast_shapes],
        scratch_shapes=[pltpu.VMEM((n_chunks // K_GROUP, K_STEP, 2 * Q_BLOCK), F32)] * 2,
        compiler_params=_params(2),
        name="attn",
    )(lam_vecs, subln_g, qt, k_all, vt_all, *cast)


def _mixout_kernel(h_ref, mod_ref, o_ref, z_ref, zp_ref, zn_ref, gb_ref, sga_ref, sgb_ref,
                   cw_ref, wpa_ref, wpb_ref, wo_ref, post_ref, out_ref, *, blocks_per_seq):
    tm = h_ref.shape[0]
    i = pl.program_id(0)

    z = z_ref[...].astype(F32)
    pos = i % blocks_per_seq
    zp = zp_ref[...].astype(F32)
    zn = zn_ref[...].astype(F32)
    prev_row = jnp.where(pos == 0, 0.0, zp[zp.shape[0] - 1:, :])
    next_row = jnp.where(pos == blocks_per_seq - 1, 0.0, zn[0:1, :])
    rows = lax.broadcasted_iota(jnp.int32, z.shape, 0)
    z_prev = jnp.where(rows == 0, prev_row, pltpu.roll(z, 1, axis=0))
    z_next = jnp.where(rows == tm - 1, next_row, pltpu.roll(z, tm - 1, axis=0))
    conv = z_prev * cw_ref[0:1, :] + z * cw_ref[1:2, :] + z_next * cw_ref[2:3, :]
    yb = (gb_ref[...].astype(F32) * conv).astype(BF16)

    half = tm // 2
    for r in range(2):
        sl = slice(r * half, (r + 1) * half)
        o = jnp.concatenate([o_ref[0, h, sl, :] for h in range(N_HEADS)], axis=1)
        y_att = _dot(o, wpa_ref[...])
        y_conv = _dot(yb[sl], wpb_ref[...])
        merged = sga_ref[sl, :].astype(F32) * y_att + sgb_ref[sl, :].astype(F32) * y_conv
        y = _dot(merged.astype(BF16), wo_ref[...])
        out_ref[sl, :] = h_ref[sl, :] + mod_ref[0, 5:6, :] * _rms(y, post_ref[...])


def _mixout(h, mods, o, z, gb, sga, sgb, conv_w, wpa, wpb, wo, post_g, *, seq):
    rows, d = h.shape
    tm = ROW_BLOCK
    bpb = seq // tm
    halo = 16
    hpb = tm // halo
    n_halo = rows // halo
    row_spec = pl.BlockSpec((tm, d), lambda i: (i, 0))
    return pl.pallas_call(
        functools.partial(_mixout_kernel, blocks_per_seq=bpb),
        grid=(rows // tm,),
        in_specs=[
            row_spec,
            pl.BlockSpec((1, N_MOD, d), lambda i: (i // bpb, 0, 0)),
            pl.BlockSpec((1, N_HEADS, tm, LANES), lambda i: (i // bpb, 0, i % bpb, 0)),
            row_spec,
            pl.BlockSpec((halo, d), lambda i: (jnp.maximum(i * hpb - 1, 0), 0)),
            pl.BlockSpec((halo, d), lambda i: (jnp.minimum((i + 1) * hpb, n_halo - 1), 0)),
            row_spec, row_spec, row_spec,
            _const_spec(conv_w.shape),
            _const_spec(wpa.shape), _const_spec(wpb.shape), _const_spec(wo.shape),
            _const_spec((1, d)),
        ],
        out_specs=row_spec,
        out_shape=jax.ShapeDtypeStruct((rows, d), F32),
        compiler_params=_params(1),
        name="mixout",
    )(h, mods, o, z, z, z, gb, sga, sgb, conv_w, wpa, wpb, wo, post_g)


def kernel(x, c, ctx, c_ctx, w_mod, b_mod, ffn1_pre_g, ffn1_post_g, ffn1_w_gate, ffn1_w_up, ffn1_w_down, mix_pre_g, mix_post_g, w_in, lam_q1, lam_k1, lam_q2, lam_k2, attn_subln_g, conv_w, w_attn_proj, w_conv_proj, w_out, ffn2_pre_g, ffn2_post_g, ffn2_w_gate, ffn2_w_up, ffn2_w_down):
    batch, seq, d = x.shape
    ctx_len = ctx.shape[1]
    assert w_mod.shape[0] == 1, "single-layer stack only"
    assert batch + 1 <= MOD_ROWS and seq % ROW_BLOCK == 0 and seq % GRID_W == 0

    cond = jnp.zeros((MOD_ROWS, d), F32).at[:batch].set(c).at[batch].set(c_ctx)
    mods = _adaln(cond, w_mod[0], b_mod[0]).reshape(MOD_ROWS, N_MOD, d)

    def row(v):
        return v.reshape(1, -1)

    ffn1_w = [w[0].astype(BF16) for w in (ffn1_w_gate, ffn1_w_up, ffn1_w_down)]

    h, w_in_b = _ffn(x.reshape(batch * seq, d), mods, row(ffn1_pre_g), row(ffn1_post_g), *ffn1_w,
                     mod_base=0, mod_row0=0, rows_per_mod_row=seq, tm=ROW_BLOCK, cast=(w_in[0],))
    hc, = _ffn(ctx.reshape(batch * ctx_len, d), mods, row(ffn1_pre_g), row(ffn1_post_g), *ffn1_w,
               mod_base=0, mod_row0=batch, rows_per_mod_row=batch * ctx_len, tm=batch * ctx_len)

    qt, k_all, vt_all, z, gb, sga, sgb = _proj(h, mods, row(mix_pre_g), w_in_b,
                                               _rope_tables(seq // GRID_W, ROW_BLOCK),
                                               batch=batch, seq=seq, n_keys=seq + ctx_len)
    k_all, vt_all = _ctx_kv(hc, mods, row(mix_pre_g), w_in_b, k_all, vt_all,
                            batch=batch, ctx_len=ctx_len, mod_row=batch)
    lam_vecs = jnp.concatenate([lam_q1, lam_k1, lam_q2, lam_k2], axis=0)
    o, wpa, wpb, wo, *ffn2_w = _attn(
        lam_vecs, attn_subln_g.reshape(V_DIM, 1), qt, k_all, vt_all,
        cast=(w_attn_proj[0], w_conv_proj[0], w_out[0], ffn2_w_gate[0], ffn2_w_up[0], ffn2_w_down[0]))
    h = _mixout(h, mods, o, z, gb, sga, sgb, conv_w[0], wpa, wpb, wo, row(mix_post_g), seq=seq)
    h, = _ffn(h, mods, row(ffn2_pre_g), row(ffn2_post_g), *ffn2_w,
              mod_base=6, mod_row0=0, rows_per_mod_row=seq, tm=FFN_ROW_BLOCK)
    return h.reshape(batch, seq, d)
```

```python
import functools
import math

import jax
import jax.numpy as jnp
from jax import lax
from jax.experimental import pallas as pl
from jax.experimental.pallas import tpu as pltpu

F32 = jnp.float32
BF16 = jnp.bfloat16

GRID_W = 64
N_HEADS = 8
HEAD_DIM = 64
V_DIM = 2 * HEAD_DIM
CONV_K = 3
N_MOD = 9
ROPE_BASE = 10000.0
AXIS_DIM = HEAD_DIM // 2
N_FREQ = AXIS_DIM // 2
EPS = 1e-6
LAM_INIT = 0.8 - 0.6 * math.exp(-0.3 * 0)
Q_SCALE = (HEAD_DIM ** -0.5) * math.log2(math.e)

LANES = 128
SUBLANES = 8
MOD_ROWS = 8
VMEM_LIMIT = 56 * 1024 * 1024

ROW_BLOCK = 512
FFN_ROW_BLOCK = 1024
FF_CHUNK = 256
FF_PART_MIN = 512
Q_BLOCK = 128
K_CHUNK = 256
K_GROUP = 3
K_STEP = K_GROUP * K_CHUNK
QB_UNROLL = 8


def _dot(a, b):
    return jnp.dot(a, b, preferred_element_type=F32)


def _const_spec(shape):
    return pl.BlockSpec(shape, lambda *_: (0,) * len(shape), pipeline_mode=pl.Buffered(1))


def _rms(x, g):
    return x * lax.rsqrt(jnp.mean(x * x, axis=-1, keepdims=True) + EPS) * g


def _params(n_grid):
    return pltpu.CompilerParams(
        dimension_semantics=("arbitrary",) * n_grid, vmem_limit_bytes=VMEM_LIMIT)


def _cast_specs(mats, steps, step_of):
    for w in mats:
        assert w.shape[0] % (steps * 2 * SUBLANES) == 0, (w.shape, steps)

    def specs():
        return [pl.BlockSpec((w.shape[0] // steps, w.shape[1]), lambda *g: (step_of(*g), 0))
                for w in mats]

    return specs(), specs(), [jax.ShapeDtypeStruct(w.shape, BF16) for w in mats]


def _cast_blocks(src_refs, dst_refs):
    for src, dst in zip(src_refs, dst_refs):
        dst[...] = src[...].astype(BF16)


def _adaln_kernel(cond_ref, w_ref, b_ref, o_ref):
    x = cond_ref[...]
    a = x * jax.nn.sigmoid(x)
    a_hi = a.astype(BF16)
    a_lo = (a - a_hi.astype(F32)).astype(BF16)
    w = w_ref[...]
    w_hi = w.astype(BF16)
    w_lo = (w - w_hi.astype(F32)).astype(BF16)
    o_ref[...] = _dot(a_hi, w_hi) + _dot(a_lo, w_hi) + _dot(a_hi, w_lo) + b_ref[...]


def _adaln(cond, w_mod, b_mod):
    d, n = w_mod.shape
    bn = n // 8
    return pl.pallas_call(
        _adaln_kernel,
        grid=(n // bn,),
        in_specs=[
            _const_spec((MOD_ROWS, d)),
            pl.BlockSpec((d, bn), lambda j: (0, j)),
            pl.BlockSpec((1, bn), lambda j: (0, j)),
        ],
        out_specs=pl.BlockSpec((MOD_ROWS, bn), lambda j: (0, j)),
        out_shape=jax.ShapeDtypeStruct((MOD_ROWS, n), F32),
        compiler_params=_params(1),
        name="adaln",
    )(cond, w_mod, b_mod.reshape(1, n))


def _ffn_kernel(x_ref, mod_ref, pre_ref, post_ref, wg_ref, wu_ref, wd_ref, *rest, mod_base, n_cast):
    cast_src, o_ref, cast_dst, a_ref = rest[:n_cast], rest[n_cast], rest[n_cast + 1:-1], rest[-1]
    _cast_blocks(cast_src, cast_dst)
    shift = mod_ref[0, mod_base:mod_base + 1, :]
    scale = mod_ref[0, mod_base + 1:mod_base + 2, :]
    gate = mod_ref[0, mod_base + 2:mod_base + 3, :]
    d_ff = wg_ref.shape[1]
    tm = x_ref.shape[0]
    n_parts = 2 if tm % (2 * FF_PART_MIN) == 0 else 1
    part = tm // n_parts
    for r in range(n_parts):
        rows = slice(r * part, (r + 1) * part)
        x = x_ref[rows, :]
        xm = (_rms(x, pre_ref[...]) * (1.0 + scale) + shift).astype(BF16)
        for c in range(d_ff // FF_CHUNK):
            sl = slice(c * FF_CHUNK, (c + 1) * FF_CHUNK)
            g = _dot(xm, wg_ref[:, sl])
            u = _dot(xm, wu_ref[:, sl])
            a_ref[rows, sl] = (g * jax.nn.sigmoid(g) * u).astype(BF16)
        y = _dot(a_ref[rows, :], wd_ref[...])
        o_ref[rows, :] = x + 0.5 * gate * _rms(y, post_ref[...])


def _ffn(x, mods, pre_g, post_g, wg, wu, wd, *, mod_base, mod_row0, rows_per_mod_row, tm, cast=()):
    rows, d = x.shape
    d_ff = wg.shape[1]
    cast_in, cast_out, cast_shapes = _cast_specs(cast, rows // tm, lambda i: i)
    assert rows_per_mod_row % tm == 0
    bpm = rows_per_mod_row // tm
    return pl.pallas_call(
        functools.partial(_ffn_kernel, mod_base=mod_base, n_cast=len(cast)),
        grid=(rows // tm,),
        in_specs=[
            pl.BlockSpec((tm, d), lambda i: (i, 0)),
            pl.BlockSpec((1, N_MOD, d), lambda i: (mod_row0 + i // bpm, 0, 0)),
            _const_spec((1, d)),
            _const_spec((1, d)),
            _const_spec((d, d_ff)),
            _const_spec((d, d_ff)),
            _const_spec((d_ff, d)),
            *cast_in,
        ],
        out_specs=[pl.BlockSpec((tm, d), lambda i: (i, 0)), *cast_out],
        out_shape=[jax.ShapeDtypeStruct((rows, d), F32), *cast_shapes],
        scratch_shapes=[pltpu.VMEM((tm, d_ff), BF16)],
        compiler_params=_params(1),
        name="ffn",
    )(x, mods, pre_g, post_g, wg, wu, wd, *cast)


def _ctx_kv_kernel(x_ref, mod_ref, g_ref, wk_ref, wv_ref, k_in_ref, vt_in_ref, k_ref, vt_ref):
    del k_in_ref, vt_in_ref
    x = x_ref[...]
    shift = mod_ref[0, 3:4, :]
    scale = mod_ref[0, 4:5, :]
    xm = (_rms(x, g_ref[...]) * (1.0 + scale) + shift).astype(BF16)
    k = _dot(xm, wk_ref[...])
    v = _dot(xm, wv_ref[...])
    for h in range(N_HEADS):
        sl = slice(h * LANES, (h + 1) * LANES)
        k_ref[0, h] = k[:, sl].astype(BF16)
        vt_ref[0, h, 0] = v[:, sl].T.astype(BF16)


def _ctx_kv(hc, mods, g, w_in, k_all, vt_all, *, batch, ctx_len, mod_row):
    d = hc.shape[1]
    assert ctx_len == K_CHUNK
    last = vt_all.shape[2] - 1
    return pl.pallas_call(
        _ctx_kv_kernel,
        grid=(batch,),
        in_specs=[
            pl.BlockSpec((ctx_len, d), lambda b: (b, 0)),
            pl.BlockSpec((1, N_MOD, d), lambda b: (mod_row, 0, 0)),
            _const_spec((1, d)),
            pl.BlockSpec((d, d), lambda b: (0, 1), pipeline_mode=pl.Buffered(1)),
            pl.BlockSpec((d, d), lambda b: (0, 2), pipeline_mode=pl.Buffered(1)),
            pl.BlockSpec(memory_space=pl.ANY),
            pl.BlockSpec(memory_space=pl.ANY),
        ],
        out_specs=[
            pl.BlockSpec((1, N_HEADS, ctx_len, LANES), lambda b: (b, 0, last, 0)),
            pl.BlockSpec((1, N_HEADS, 1, LANES, ctx_len), lambda b: (b, 0, last, 0, 0)),
        ],
        out_shape=[
            jax.ShapeDtypeStruct(k_all.shape, BF16),
            jax.ShapeDtypeStruct(vt_all.shape, BF16),
        ],
        input_output_aliases={5: 0, 6: 1},
        compiler_params=_params(1),
        name="ctx_kv",
    )(hc, mods, g, w_in, w_in, k_all, vt_all)


def _rope_tables(n_rows, tm):
    inv_freq = ROPE_BASE ** (-2.0 * jnp.arange(N_FREQ, dtype=F32) / AXIS_DIM)
    lane = jnp.arange(LANES)
    row_lane = (lane % HEAD_DIM) < AXIS_DIM
    first_half = (lane % AXIS_DIM) < N_FREQ

    def factors(pos, keep):
        ang = jnp.tile(pos.astype(F32)[:, None] * inv_freq, (1, LANES // N_FREQ))
        cos, sin = jnp.cos(ang), jnp.sin(ang)
        tabs = (cos, jnp.where(first_half, -sin, 0.0), jnp.where(first_half, 0.0, sin))
        return [jnp.where(keep, t, 0.0) for t in tabs]

    return (factors(jnp.arange(n_rows), row_lane)
            + factors(jnp.arange(tm) % GRID_W, jnp.logical_not(row_lane)))


def _proj_kernel(x_ref, mod_ref, g_ref, w_ref, rcos_ref, rsup_ref, rsdn_ref, ccos_ref, csup_ref, csdn_ref,
                 qt_ref, k_ref, vt_ref, z_ref, gb_ref, sga_ref, sgb_ref):
    tm, d = x_ref.shape
    x = x_ref[...]
    shift = mod_ref[0, 3:4, :]
    scale = mod_ref[0, 4:5, :]
    xm = (_rms(x, g_ref[...]) * (1.0 + scale) + shift).astype(BF16)

    def table(row_ref, col_ref):
        rt = row_ref[...]
        rows = [jnp.broadcast_to(rt[r:r + 1, :], (GRID_W, LANES)) for r in range(tm // GRID_W)]
        return jnp.concatenate(rows, axis=0) + col_ref[...]

    cos, sup, sdn = table(rcos_ref, ccos_ref), table(rsup_ref, csup_ref), table(rsdn_ref, csdn_ref)

    def group(j):
        return _dot(xm, w_ref[:, j * d:(j + 1) * d])

    def rope(xh):
        return (xh * cos + pltpu.roll(xh, LANES - N_FREQ, axis=1) * sup
                + pltpu.roll(xh, N_FREQ, axis=1) * sdn)

    uq = group(0)
    for h in range(N_HEADS):
        qh = rope(uq[:, h * LANES:(h + 1) * LANES]) * Q_SCALE
        for r in range(tm // Q_BLOCK):
            qt_ref[0, h, r] = qh[r * Q_BLOCK:(r + 1) * Q_BLOCK, :].T.astype(BF16)
    uk = group(1)
    for h in range(N_HEADS):
        k_ref[0, h] = rope(uk[:, h * LANES:(h + 1) * LANES]).astype(BF16)
    uv = group(2)
    for h in range(N_HEADS):
        vh = uv[:, h * LANES:(h + 1) * LANES]
        for r in range(tm // K_CHUNK):
            vt_ref[0, h, r] = vh[r * K_CHUNK:(r + 1) * K_CHUNK, :].T.astype(BF16)
    gb_ref[...] = group(3).astype(BF16)
    z_ref[...] = (group(4) * group(5)).astype(BF16)
    sga_ref[...] = jax.nn.sigmoid(group(6)).astype(BF16)
    sgb_ref[...] = jax.nn.sigmoid(group(7)).astype(BF16)


def _proj(h, mods, g, w_in, tables, *, batch, seq, n_keys):
    rows, d = h.shape
    tm = ROW_BLOCK
    bpb = seq // tm
    qpb, cpb = tm // Q_BLOCK, tm // K_CHUNK
    row_spec = pl.BlockSpec((tm, d), lambda i: (i, 0))
    row_out = jax.ShapeDtypeStruct((rows, d), BF16)
    row_tab = pl.BlockSpec((tm // GRID_W, LANES), lambda i: (i % bpb, 0))
    col_tab = _const_spec((tm, LANES))
    return pl.pallas_call(
        _proj_kernel,
        grid=(rows // tm,),
        in_specs=[
            row_spec,
            pl.BlockSpec((1, N_MOD, d), lambda i: (i // bpb, 0, 0)),
            _const_spec((1, d)),
            _const_spec(w_in.shape),
            row_tab, row_tab, row_tab, col_tab, col_tab, col_tab,
        ],
        out_specs=[
            pl.BlockSpec((1, N_HEADS, qpb, LANES, Q_BLOCK), lambda i: (i // bpb, 0, i % bpb, 0, 0)),
            pl.BlockSpec((1, N_HEADS, tm, LANES), lambda i: (i // bpb, 0, i % bpb, 0)),
            pl.BlockSpec((1, N_HEADS, cpb, LANES, K_CHUNK), lambda i: (i // bpb, 0, i % bpb, 0, 0)),
            row_spec, row_spec, row_spec, row_spec,
        ],
        out_shape=[
            jax.ShapeDtypeStruct((batch, N_HEADS, seq // Q_BLOCK, LANES, Q_BLOCK), BF16),
            jax.ShapeDtypeStruct((batch, N_HEADS, n_keys, LANES), BF16),
            jax.ShapeDtypeStruct((batch, N_HEADS, n_keys // K_CHUNK, LANES, K_CHUNK), BF16),
            row_out, row_out, row_out, row_out,
        ],
        compiler_params=_params(1),
        name="proj",
    )(h, mods, g, w_in, *tables)


def _attn_kernel(lam_ref, g_ref, qt_ref, k_ref, vt_ref, *rest, n_qblocks, n_chunks, n_cast):
    cast_src, o_ref, cast_dst = rest[:n_cast], rest[n_cast], rest[n_cast + 1:-2]
    s0_ref, s1_ref = rest[-2:]
    _cast_blocks(cast_src, cast_dst)
    tq = Q_BLOCK
    two_tq = 2 * tq
    s_refs = (s0_ref, s1_ref)

    lv = lam_ref[...]
    lam = (jnp.exp(jnp.sum(lv[0:1] * lv[1:2], axis=-1, keepdims=True))
           - jnp.exp(jnp.sum(lv[2:3] * lv[3:4], axis=-1, keepdims=True)) + LAM_INIT)

    def load_qq(i):
        qt = qt_ref[0, 0, i]
        feat = lax.broadcasted_iota(jnp.int32, qt.shape, 0)
        zero = jnp.zeros_like(qt)
        return jnp.concatenate(
            [jnp.where(feat < HEAD_DIM, qt, zero), jnp.where(feat >= HEAD_DIM, qt, zero)], axis=1)

    def fold(x, op):
        return op(x.reshape(K_STEP // SUBLANES, SUBLANES, two_tq), axis=0)

    def stage_a(j, s_ref, qq, mx):
        s = _dot(k_ref[0, 0, j * K_STEP:(j + 1) * K_STEP, :], qq)
        s_ref[j] = s
        return jnp.maximum(mx, fold(s, jnp.max))

    def stage_b(j, s_ref, m, ls, acc):
        e = jnp.exp2(s_ref[j] - m)
        ls = ls + fold(e, jnp.sum)
        vt = jnp.concatenate([vt_ref[0, 0, K_GROUP * j + t] for t in range(K_GROUP)], axis=1)
        acc = acc + _dot(vt, e.astype(BF16))
        return ls, acc

    mx0 = jnp.full((SUBLANES, two_tq), -jnp.inf, F32)
    ls0 = jnp.zeros((SUBLANES, two_tq), F32)
    acc0 = jnp.zeros((LANES, two_tq), F32)

    def finalize(i, ls, acc):
        l = jnp.sum(ls, axis=0, keepdims=True)
        o = acc[:, :tq] * (1.0 / l[:, :tq]) - acc[:, tq:] * (lam / l[:, tq:])
        on = o * lax.rsqrt(jnp.mean(o * o, axis=0, keepdims=True) + EPS) * g_ref[...] * (1.0 - LAM_INIT)
        off = pl.multiple_of(i * tq, tq)
        o_ref[0, 0, pl.ds(off, tq), :] = on.T.astype(BF16)


    def run_a(i, parity):
        qq = load_qq(i)
        mx = mx0
        for j in range(n_chunks):
            mx = stage_a(j, s_refs[parity], qq, mx)
        return jnp.max(mx, axis=0, keepdims=True)

    def run_b(parity, m):
        ls, acc = ls0, acc0
        for j in range(n_chunks):
            ls, acc = stage_b(j, s_refs[parity], m, ls, acc)
        return ls, acc

    def run_ab(i, parity, m_prev):
        qq = load_qq(i)
        mx, ls, acc = mx0, ls0, acc0
        for j in range(n_chunks):
            mx = stage_a(j, s_refs[parity], qq, mx)
            ls, acc = stage_b(j, s_refs[1 - parity], m_prev, ls, acc)
        return jnp.max(mx, axis=0, keepdims=True), ls, acc

    def blocks(first, count, carry):
        m, ls, acc = carry
        for t in range(count):
            finalize(first + t - 2, ls, acc)
            m, ls, acc = run_ab(first + t, t % 2, m)
        return m, ls, acc

    assert n_qblocks % QB_UNROLL == 0 and QB_UNROLL % 2 == 0 and n_qblocks >= 2 * QB_UNROLL
    m = run_a(0, 0)
    carry = run_ab(1, 1, m)
    carry = blocks(2, QB_UNROLL - 2, carry)
    m, ls, acc = lax.fori_loop(1, n_qblocks // QB_UNROLL,
                               lambda p, c: blocks(QB_UNROLL * p, QB_UNROLL, c), carry)
    finalize(n_qblocks - 2, ls, acc)
    finalize(n_qblocks - 1, *run_b(1, m))


def _attn(lam_vecs, subln_g, qt, k_all, vt_all, cast=()):
    batch, heads, n_qblocks = qt.shape[:3]
    cast_in, cast_out, cast_shapes = _cast_specs(cast, batch * heads, lambda b, h: b * heads + h)
    n_keys = k_all.shape[2]
    n_chunks = n_keys // K_CHUNK
    assert n_chunks % K_GROUP == 0
    seq = n_qblocks * Q_BLOCK
    return pl.pallas_call(
        functools.partial(_attn_kernel, n_qblocks=n_qblocks, n_chunks=n_chunks // K_GROUP,
                          n_cast=len(cast)),
        grid=(batch, heads),
        in_specs=[
            _const_spec(lam_vecs.shape),
            _const_spec(subln_g.shape),
            pl.BlockSpec((1, 1, n_qblocks, LANES, Q_BLOCK), lambda b, h: (b, h, 0, 0, 0)),
            pl.BlockSpec((1, 1, n_keys, LANES), lambda b, h: (b, h, 0, 0)),
            pl.BlockSpec((1, 1, n_chunks, LANES, K_CHUNK), lambda b, h: (b, h, 0, 0, 0)),
            *cast_in,
        ],
        out_specs=[pl.BlockSpec((1, 1, seq, LANES), lambda b, h: (b, h, 0, 0)), *cast_out],
        out_shape=[jax.ShapeDtypeStruct((batch, heads, seq, LANES), BF16), *cast_shapes],
        scratch_shapes=[pltpu.VMEM((n_chunks // K_GROUP, K_STEP, 2 * Q_BLOCK), F32)] * 2,
        compiler_params=_params(2),
        name="attn",
    )(lam_vecs, subln_g, qt, k_all, vt_all, *cast)


def _mixout_kernel(h_ref, mod_ref, o_ref, z_ref, zp_ref, zn_ref, gb_ref, sga_ref, sgb_ref,
                   cw_ref, wpa_ref, wpb_ref, wo_ref, post_ref, out_ref, *, blocks_per_seq):
    tm = h_ref.shape[0]
    i = pl.program_id(0)

    z = z_ref[...].astype(F32)
    pos = i % blocks_per_seq
    zp = zp_ref[...].astype(F32)
    zn = zn_ref[...].astype(F32)
    prev_row = jnp.where(pos == 0, 0.0, zp[zp.shape[0] - 1:, :])
    next_row = jnp.where(pos == blocks_per_seq - 1, 0.0, zn[0:1, :])
    rows = lax.broadcasted_iota(jnp.int32, z.shape, 0)
    z_prev = jnp.where(rows == 0, prev_row, pltpu.roll(z, 1, axis=0))
    z_next = jnp.where(rows == tm - 1, next_row, pltpu.roll(z, tm - 1, axis=0))
    conv = z_prev * cw_ref[0:1, :] + z * cw_ref[1:2, :] + z_next * cw_ref[2:3, :]
    yb = (gb_ref[...].astype(F32) * conv).astype(BF16)

    half = tm // 2
    for r in range(2):
        sl = slice(r * half, (r + 1) * half)
        o = jnp.concatenate([o_ref[0, h, sl, :] for h in range(N_HEADS)], axis=1)
        y_att = _dot(o, wpa_ref[...])
        y_conv = _dot(yb[sl], wpb_ref[...])
        merged = sga_ref[sl, :].astype(F32) * y_att + sgb_ref[sl, :].astype(F32) * y_conv
        y = _dot(merged.astype(BF16), wo_ref[...])
        out_ref[sl, :] = h_ref[sl, :] + mod_ref[0, 5:6, :] * _rms(y, post_ref[...])


def _mixout(h, mods, o, z, gb, sga, sgb, conv_w, wpa, wpb, wo, post_g, *, seq):
    rows, d = h.shape
    tm = ROW_BLOCK
    bpb = seq // tm
    halo = 16
    hpb = tm // halo
    n_halo = rows // halo
    row_spec = pl.BlockSpec((tm, d), lambda i: (i, 0))
    return pl.pallas_call(
        functools.partial(_mixout_kernel, blocks_per_seq=bpb),
        grid=(rows // tm,),
        in_specs=[
            row_spec,
            pl.BlockSpec((1, N_MOD, d), lambda i: (i // bpb, 0, 0)),
            pl.BlockSpec((1, N_HEADS, tm, LANES), lambda i: (i // bpb, 0, i % bpb, 0)),
            row_spec,
            pl.BlockSpec((halo, d), lambda i: (jnp.maximum(i * hpb - 1, 0), 0)),
            pl.BlockSpec((halo, d), lambda i: (jnp.minimum((i + 1) * hpb, n_halo - 1), 0)),
            row_spec, row_spec, row_spec,
            _const_spec(conv_w.shape),
            _const_spec(wpa.shape), _const_spec(wpb.shape), _const_spec(wo.shape),
            _const_spec((1, d)),
        ],
        out_specs=row_spec,
        out_shape=jax.ShapeDtypeStruct((rows, d), F32),
        compiler_params=_params(1),
        name="mixout",
    )(h, mods, o, z, z, z, gb, sga, sgb, conv_w, wpa, wpb, wo, post_g)


def kernel(x, c, ctx, c_ctx, w_mod, b_mod, ffn1_pre_g, ffn1_post_g, ffn1_w_gate, ffn1_w_up, ffn1_w_down, mix_pre_g, mix_post_g, w_in, lam_q1, lam_k1, lam_q2, lam_k2, attn_subln_g, conv_w, w_attn_proj, w_conv_proj, w_out, ffn2_pre_g, ffn2_post_g, ffn2_w_gate, ffn2_w_up, ffn2_w_down):
    batch, seq, d = x.shape
    ctx_len = ctx.shape[1]
    assert w_mod.shape[0] == 1, "single-layer stack only"
    assert batch + 1 <= MOD_ROWS and seq % ROW_BLOCK == 0 and seq % GRID_W == 0

    cond = jnp.zeros((MOD_ROWS, d), F32).at[:batch].set(c).at[batch].set(c_ctx)
    mods = _adaln(cond, w_mod[0], b_mod[0]).reshape(MOD_ROWS, N_MOD, d)

    def row(v):
        return v.reshape(1, -1)

    ffn1_w = [w[0].astype(BF16) for w in (ffn1_w_gate, ffn1_w_up, ffn1_w_down)]

    h, w_in_b, wpa, wpb, wo = _ffn(
        x.reshape(batch * seq, d), mods, row(ffn1_pre_g), row(ffn1_post_g), *ffn1_w,
        mod_base=0, mod_row0=0, rows_per_mod_row=seq, tm=FFN_ROW_BLOCK,
        cast=(w_in[0], w_attn_proj[0], w_conv_proj[0], w_out[0]))
    hc, = _ffn(ctx.reshape(batch * ctx_len, d), mods, row(ffn1_pre_g), row(ffn1_post_g), *ffn1_w,
               mod_base=0, mod_row0=batch, rows_per_mod_row=batch * ctx_len, tm=batch * ctx_len)

    qt, k_all, vt_all, z, gb, sga, sgb = _proj(h, mods, row(mix_pre_g), w_in_b,
                                               _rope_tables(seq // GRID_W, ROW_BLOCK),
                                               batch=batch, seq=seq, n_keys=seq + ctx_len)
    k_all, vt_all = _ctx_kv(hc, mods, row(mix_pre_g), w_in_b, k_all, vt_all,
                            batch=batch, ctx_len=ctx_len, mod_row=batch)
    lam_vecs = jnp.concatenate([lam_q1, lam_k1, lam_q2, lam_k2], axis=0)
    o, *ffn2_w = _attn(lam_vecs, attn_subln_g.reshape(V_DIM, 1), qt, k_all, vt_all,
                       cast=(ffn2_w_gate[0], ffn2_w_up[0], ffn2_w_down[0]))
    h = _mixout(h, mods, o, z, gb, sga, sgb, conv_w[0], wpa, wpb, wo, row(mix_post_g), seq=seq)
    h, = _ffn(h, mods, row(ffn2_pre_g), row(ffn2_post_g), *ffn2_w,
              mod_base=6, mod_row0=0, rows_per_mod_row=seq, tm=FFN_ROW_BLOCK)
    return h.reshape(batch, seq, d)
```

```python
import functools
import math

import jax
import jax.numpy as jnp
from jax import lax
from jax.experimental import pallas as pl
from jax.experimental.pallas import tpu as pltpu

F32 = jnp.float32
BF16 = jnp.bfloat16

GRID_W = 64
N_HEADS = 8
HEAD_DIM = 64
V_DIM = 2 * HEAD_DIM
CONV_K = 3
N_MOD = 9
ROPE_BASE = 10000.0
AXIS_DIM = HEAD_DIM // 2
N_FREQ = AXIS_DIM // 2
EPS = 1e-6
LAM_INIT = 0.8 - 0.6 * math.exp(-0.3 * 0)
Q_SCALE = (HEAD_DIM ** -0.5) * math.log2(math.e)

LANES = 128
SUBLANES = 8
MOD_ROWS = 8
VMEM_LIMIT = 56 * 1024 * 1024

ROW_BLOCK = 512
FFN_ROW_BLOCK = 1024
FF_CHUNK = 256
FF_PART_MIN = 512
Q_BLOCK = 128
K_CHUNK = 256
K_GROUP = 3
K_STEP = K_GROUP * K_CHUNK
QB_UNROLL = 4


def _dot(a, b):
    return jnp.dot(a, b, preferred_element_type=F32)


def _const_spec(shape):
    return pl.BlockSpec(shape, lambda *_: (0,) * len(shape), pipeline_mode=pl.Buffered(1))


def _rms(x, g):
    return x * lax.rsqrt(jnp.mean(x * x, axis=-1, keepdims=True) + EPS) * g


def _params(n_grid):
    return pltpu.CompilerParams(
        dimension_semantics=("arbitrary",) * n_grid, vmem_limit_bytes=VMEM_LIMIT)


def _cast_specs(mats, steps, step_of):
    for w in mats:
        assert w.shape[0] % (steps * 2 * SUBLANES) == 0, (w.shape, steps)

    def specs():
        return [pl.BlockSpec((w.shape[0] // steps, w.shape[1]), lambda *g: (step_of(*g), 0))
                for w in mats]

    return specs(), specs(), [jax.ShapeDtypeStruct(w.shape, BF16) for w in mats]


def _cast_blocks(src_refs, dst_refs):
    for src, dst in zip(src_refs, dst_refs):
        dst[...] = src[...].astype(BF16)


def _adaln_kernel(cond_ref, w_ref, b_ref, o_ref):
    x = cond_ref[...]
    a = x * jax.nn.sigmoid(x)
    a_hi = a.astype(BF16)
    a_lo = (a - a_hi.astype(F32)).astype(BF16)
    w = w_ref[...]
    w_hi = w.astype(BF16)
    w_lo = (w - w_hi.astype(F32)).astype(BF16)
    o_ref[...] = _dot(a_hi, w_hi) + _dot(a_lo, w_hi) + _dot(a_hi, w_lo) + b_ref[...]


def _adaln(cond, w_mod, b_mod):
    d, n = w_mod.shape
    bn = n // 8
    return pl.pallas_call(
        _adaln_kernel,
        grid=(n // bn,),
        in_specs=[
            _const_spec((MOD_ROWS, d)),
            pl.BlockSpec((d, bn), lambda j: (0, j)),
            pl.BlockSpec((1, bn), lambda j: (0, j)),
        ],
        out_specs=pl.BlockSpec((MOD_ROWS, bn), lambda j: (0, j)),
        out_shape=jax.ShapeDtypeStruct((MOD_ROWS, n), F32),
        compiler_params=_params(1),
        name="adaln",
    )(cond, w_mod, b_mod.reshape(1, n))


def _ffn_kernel(x_ref, mod_ref, pre_ref, post_ref, wg_ref, wu_ref, wd_ref, *rest, mod_base, n_cast):
    cast_src, o_ref, cast_dst, a_ref = rest[:n_cast], rest[n_cast], rest[n_cast + 1:-1], rest[-1]
    _cast_blocks(cast_src, cast_dst)
    shift = mod_ref[0, mod_base:mod_base + 1, :]
    scale = mod_ref[0, mod_base + 1:mod_base + 2, :]
    gate = mod_ref[0, mod_base + 2:mod_base + 3, :]
    d_ff = wg_ref.shape[1]
    tm = x_ref.shape[0]
    n_parts = 2 if tm % (2 * FF_PART_MIN) == 0 else 1
    part = tm // n_parts
    for r in range(n_parts):
        rows = slice(r * part, (r + 1) * part)
        x = x_ref[rows, :]
        xm = (_rms(x, pre_ref[...]) * (1.0 + scale) + shift).astype(BF16)
        for c in range(d_ff // FF_CHUNK):
            sl = slice(c * FF_CHUNK, (c + 1) * FF_CHUNK)
            g = _dot(xm, wg_ref[:, sl])
            u = _dot(xm, wu_ref[:, sl])
            a_ref[rows, sl] = (g * jax.nn.sigmoid(g) * u).astype(BF16)
        y = _dot(a_ref[rows, :], wd_ref[...])
        o_ref[rows, :] = x + 0.5 * gate * _rms(y, post_ref[...])


def _ffn(x, mods, pre_g, post_g, wg, wu, wd, *, mod_base, mod_row0, rows_per_mod_row, tm, cast=()):
    rows, d = x.shape
    d_ff = wg.shape[1]
    cast_in, cast_out, cast_shapes = _cast_specs(cast, rows // tm, lambda i: i)
    assert rows_per_mod_row % tm == 0
    bpm = rows_per_mod_row // tm
    return pl.pallas_call(
        functools.partial(_ffn_kernel, mod_base=mod_base, n_cast=len(cast)),
        grid=(rows // tm,),
        in_specs=[
            pl.BlockSpec((tm, d), lambda i: (i, 0)),
            pl.BlockSpec((1, N_MOD, d), lambda i: (mod_row0 + i // bpm, 0, 0)),
            _const_spec((1, d)),
            _const_spec((1, d)),
            _const_spec((d, d_ff)),
            _const_spec((d, d_ff)),
            _const_spec((d_ff, d)),
            *cast_in,
        ],
        out_specs=[pl.BlockSpec((tm, d), lambda i: (i, 0)), *cast_out],
        out_shape=[jax.ShapeDtypeStruct((rows, d), F32), *cast_shapes],
        scratch_shapes=[pltpu.VMEM((tm, d_ff), BF16)],
        compiler_params=_params(1),
        name="ffn",
    )(x, mods, pre_g, post_g, wg, wu, wd, *cast)


def _ctx_kv_kernel(x_ref, mod_ref, g_ref, wk_ref, wv_ref, k_in_ref, vt_in_ref, k_ref, vt_ref):
    del k_in_ref, vt_in_ref
    x = x_ref[...]
    shift = mod_ref[0, 3:4, :]
    scale = mod_ref[0, 4:5, :]
    xm = (_rms(x, g_ref[...]) * (1.0 + scale) + shift).astype(BF16)
    k = _dot(xm, wk_ref[...])
    v = _dot(xm, wv_ref[...])
    for h in range(N_HEADS):
        sl = slice(h * LANES, (h + 1) * LANES)
        k_ref[0, h] = k[:, sl].astype(BF16)
        vt_ref[0, h, 0] = v[:, sl].T.astype(BF16)


def _ctx_kv(hc, mods, g, w_in, k_all, vt_all, *, batch, ctx_len, mod_row):
    d = hc.shape[1]
    assert ctx_len == K_CHUNK
    last = vt_all.shape[2] - 1
    return pl.pallas_call(
        _ctx_kv_kernel,
        grid=(batch,),
        in_specs=[
            pl.BlockSpec((ctx_len, d), lambda b: (b, 0)),
            pl.BlockSpec((1, N_MOD, d), lambda b: (mod_row, 0, 0)),
            _const_spec((1, d)),
            pl.BlockSpec((d, d), lambda b: (0, 1), pipeline_mode=pl.Buffered(1)),
            pl.BlockSpec((d, d), lambda b: (0, 2), pipeline_mode=pl.Buffered(1)),
            pl.BlockSpec(memory_space=pl.ANY),
            pl.BlockSpec(memory_space=pl.ANY),
        ],
        out_specs=[
            pl.BlockSpec((1, N_HEADS, ctx_len, LANES), lambda b: (b, 0, last, 0)),
            pl.BlockSpec((1, N_HEADS, 1, LANES, ctx_len), lambda b: (b, 0, last, 0, 0)),
        ],
        out_shape=[
            jax.ShapeDtypeStruct(k_all.shape, BF16),
            jax.ShapeDtypeStruct(vt_all.shape, BF16),
        ],
        input_output_aliases={5: 0, 6: 1},
        compiler_params=_params(1),
        name="ctx_kv",
    )(hc, mods, g, w_in, w_in, k_all, vt_all)


def _rope_tables(n_rows, tm):
    inv_freq = ROPE_BASE ** (-2.0 * jnp.arange(N_FREQ, dtype=F32) / AXIS_DIM)
    lane = jnp.arange(LANES)
    row_lane = (lane % HEAD_DIM) < AXIS_DIM
    first_half = (lane % AXIS_DIM) < N_FREQ

    def factors(pos, keep):
        ang = jnp.tile(pos.astype(F32)[:, None] * inv_freq, (1, LANES // N_FREQ))
        cos, sin = jnp.cos(ang), jnp.sin(ang)
        tabs = (cos, jnp.where(first_half, -sin, 0.0), jnp.where(first_half, 0.0, sin))
        return [jnp.where(keep, t, 0.0) for t in tabs]

    return (factors(jnp.arange(n_rows), row_lane)
            + factors(jnp.arange(tm) % GRID_W, jnp.logical_not(row_lane)))


def _proj_kernel(x_ref, mod_ref, g_ref, w_ref, rcos_ref, rsup_ref, rsdn_ref, ccos_ref, csup_ref, csdn_ref,
                 qt_ref, k_ref, vt_ref, z_ref, gb_ref, sga_ref, sgb_ref):
    tm, d = x_ref.shape
    x = x_ref[...]
    shift = mod_ref[0, 3:4, :]
    scale = mod_ref[0, 4:5, :]
    xm = (_rms(x, g_ref[...]) * (1.0 + scale) + shift).astype(BF16)

    def table(row_ref, col_ref):
        rt = row_ref[...]
        rows = [jnp.broadcast_to(rt[r:r + 1, :], (GRID_W, LANES)) for r in range(tm // GRID_W)]
        return jnp.concatenate(rows, axis=0) + col_ref[...]

    cos, sup, sdn = table(rcos_ref, ccos_ref), table(rsup_ref, csup_ref), table(rsdn_ref, csdn_ref)

    def group(j):
        return _dot(xm, w_ref[:, j * d:(j + 1) * d])

    def rope(xh):
        return (xh * cos + pltpu.roll(xh, LANES - N_FREQ, axis=1) * sup
                + pltpu.roll(xh, N_FREQ, axis=1) * sdn)

    uq = group(0)
    for h in range(N_HEADS):
        qh = rope(uq[:, h * LANES:(h + 1) * LANES]) * Q_SCALE
        for r in range(tm // Q_BLOCK):
            qt_ref[0, h, r] = qh[r * Q_BLOCK:(r + 1) * Q_BLOCK, :].T.astype(BF16)
    uk = group(1)
    for h in range(N_HEADS):
        k_ref[0, h] = rope(uk[:, h * LANES:(h + 1) * LANES]).astype(BF16)
    uv = group(2)
    for h in range(N_HEADS):
        vh = uv[:, h * LANES:(h + 1) * LANES]
        for r in range(tm // K_CHUNK):
            vt_ref[0, h, r] = vh[r * K_CHUNK:(r + 1) * K_CHUNK, :].T.astype(BF16)
    gb_ref[...] = group(3).astype(BF16)
    z_ref[...] = (group(4) * group(5)).astype(BF16)
    sga_ref[...] = jax.nn.sigmoid(group(6)).astype(BF16)
    sgb_ref[...] = jax.nn.sigmoid(group(7)).astype(BF16)


def _proj(h, mods, g, w_in, tables, *, batch, seq, n_keys):
    rows, d = h.shape
    tm = ROW_BLOCK
    bpb = seq // tm
    qpb, cpb = tm // Q_BLOCK, tm // K_CHUNK
    row_spec = pl.BlockSpec((tm, d), lambda i: (i, 0))
    row_out = jax.ShapeDtypeStruct((rows, d), BF16)
    row_tab = pl.BlockSpec((tm // GRID_W, LANES), lambda i: (i % bpb, 0))
    col_tab = _const_spec((tm, LANES))
    return pl.pallas_call(
        _proj_kernel,
        grid=(rows // tm,),
        in_specs=[
            row_spec,
            pl.BlockSpec((1, N_MOD, d), lambda i: (i // bpb, 0, 0)),
            _const_spec((1, d)),
            _const_spec(w_in.shape),
            row_tab, row_tab, row_tab, col_tab, col_tab, col_tab,
        ],
        out_specs=[
            pl.BlockSpec((1, N_HEADS, qpb, LANES, Q_BLOCK), lambda i: (i // bpb, 0, i % bpb, 0, 0)),
            pl.BlockSpec((1, N_HEADS, tm, LANES), lambda i: (i // bpb, 0, i % bpb, 0)),
            pl.BlockSpec((1, N_HEADS, cpb, LANES, K_CHUNK), lambda i: (i // bpb, 0, i % bpb, 0, 0)),
            row_spec, row_spec, row_spec, row_spec,
        ],
        out_shape=[
            jax.ShapeDtypeStruct((batch, N_HEADS, seq // Q_BLOCK, LANES, Q_BLOCK), BF16),
            jax.ShapeDtypeStruct((batch, N_HEADS, n_keys, LANES), BF16),
            jax.ShapeDtypeStruct((batch, N_HEADS, n_keys // K_CHUNK, LANES, K_CHUNK), BF16),
            row_out, row_out, row_out, row_out,
        ],
        compiler_params=_params(1),
        name="proj",
    )(h, mods, g, w_in, *tables)


def _attn_kernel(lam_ref, g_ref, qt_ref, k_ref, vt_ref, *rest, n_qblocks, n_chunks, n_cast):
    cast_src, o_ref, cast_dst = rest[:n_cast], rest[n_cast], rest[n_cast + 1:-2]
    s0_ref, s1_ref = rest[-2:]
    _cast_blocks(cast_src, cast_dst)
    tq = Q_BLOCK
    two_tq = 2 * tq
    s_refs = (s0_ref, s1_ref)

    lv = lam_ref[...]
    lam = (jnp.exp(jnp.sum(lv[0:1] * lv[1:2], axis=-1, keepdims=True))
           - jnp.exp(jnp.sum(lv[2:3] * lv[3:4], axis=-1, keepdims=True)) + LAM_INIT)

    def load_qq(i):
        qt = qt_ref[0, 0, i]
        feat = lax.broadcasted_iota(jnp.int32, qt.shape, 0)
        zero = jnp.zeros_like(qt)
        return jnp.concatenate(
            [jnp.where(feat < HEAD_DIM, qt, zero), jnp.where(feat >= HEAD_DIM, qt, zero)], axis=1)

    def fold(x, op):
        return op(x.reshape(K_STEP // SUBLANES, SUBLANES, two_tq), axis=0)

    def stage_a(j, s_ref, qq, mx):
        s = _dot(k_ref[0, 0, j * K_STEP:(j + 1) * K_STEP, :], qq)
        s_ref[j] = s
        return jnp.maximum(mx, fold(s, jnp.max))

    def stage_b(j, s_ref, m, ls, acc):
        e = jnp.exp2(s_ref[j] - m)
        ls = ls + fold(e, jnp.sum)
        vt = jnp.concatenate([vt_ref[0, 0, K_GROUP * j + t] for t in range(K_GROUP)], axis=1)
        acc = acc + _dot(vt, e.astype(BF16))
        return ls, acc

    mx0 = jnp.full((SUBLANES, two_tq), -jnp.inf, F32)
    ls0 = jnp.zeros((SUBLANES, two_tq), F32)
    acc0 = jnp.zeros((LANES, two_tq), F32)

    def finalize(i, ls, acc):
        l = jnp.sum(ls, axis=0, keepdims=True)
        o = acc[:, :tq] * (1.0 / l[:, :tq]) - acc[:, tq:] * (lam / l[:, tq:])
        on = o * lax.rsqrt(jnp.mean(o * o, axis=0, keepdims=True) + EPS) * g_ref[...] * (1.0 - LAM_INIT)
        off = pl.multiple_of(i * tq, tq)
        o_ref[0, 0, pl.ds(off, tq), :] = on.T.astype(BF16)


    def run_a(i, parity):
        qq = load_qq(i)
        mx = mx0
        for j in range(n_chunks):
            mx = stage_a(j, s_refs[parity], qq, mx)
        return jnp.max(mx, axis=0, keepdims=True)

    def run_b(parity, m):
        ls, acc = ls0, acc0
        for j in range(n_chunks):
            ls, acc = stage_b(j, s_refs[parity], m, ls, acc)
        return ls, acc

    def run_ab(i, parity, m_prev):
        qq = load_qq(i)
        mx, ls, acc = mx0, ls0, acc0
        for j in range(n_chunks):
            mx = stage_a(j, s_refs[parity], qq, mx)
            ls, acc = stage_b(j, s_refs[1 - parity], m_prev, ls, acc)
        return jnp.max(mx, axis=0, keepdims=True), ls, acc

    def blocks(first, count, carry):
        m, ls, acc = carry
        for t in range(count):
            finalize(first + t - 2, ls, acc)
            m, ls, acc = run_ab(first + t, t % 2, m)
        return m, ls, acc

    assert n_qblocks % QB_UNROLL == 0 and QB_UNROLL % 2 == 0 and n_qblocks >= 2 * QB_UNROLL
    m = run_a(0, 0)
    carry = run_ab(1, 1, m)
    carry = blocks(2, QB_UNROLL - 2, carry)
    m, ls, acc = lax.fori_loop(1, n_qblocks // QB_UNROLL,
                               lambda p, c: blocks(QB_UNROLL * p, QB_UNROLL, c), carry)
    finalize(n_qblocks - 2, ls, acc)
    finalize(n_qblocks - 1, *run_b(1, m))


def _attn(lam_vecs, subln_g, qt, k_all, vt_all, cast=()):
    batch, heads, n_qblocks = qt.shape[:3]
    cast_in, cast_out, cast_shapes = _cast_specs(cast, batch * heads, lambda b, h: b * heads + h)
    n_keys = k_all.shape[2]
    n_chunks = n_keys // K_CHUNK
    assert n_chunks % K_GROUP == 0
    seq = n_qblocks * Q_BLOCK
    return pl.pallas_call(
        functools.partial(_attn_kernel, n_qblocks=n_qblocks, n_chunks=n_chunks // K_GROUP,
                          n_cast=len(cast)),
        grid=(batch, heads),
        in_specs=[
            _const_spec(lam_vecs.shape),
            _const_spec(subln_g.shape),
            pl.BlockSpec((1, 1, n_qblocks, LANES, Q_BLOCK), lambda b, h: (b, h, 0, 0, 0)),
            pl.BlockSpec((1, 1, n_keys, LANES), lambda b, h: (b, h, 0, 0)),
            pl.BlockSpec((1, 1, n_chunks, LANES, K_CHUNK), lambda b, h: (b, h, 0, 0, 0)),
            *cast_in,
        ],
        out_specs=[pl.BlockSpec((1, 1, seq, LANES), lambda b, h: (b, h, 0, 0)), *cast_out],
        out_shape=[jax.ShapeDtypeStruct((batch, heads, seq, LANES), BF16), *cast_shapes],
        scratch_shapes=[pltpu.VMEM((n_chunks // K_GROUP, K_STEP, 2 * Q_BLOCK), F32)] * 2,
        compiler_params=_params(2),
        name="attn",
    )(lam_vecs, subln_g, qt, k_all, vt_all, *cast)


def _mixout_kernel(h_ref, mod_ref, o_ref, z_ref, zp_ref, zn_ref, gb_ref, sga_ref, sgb_ref,
                   cw_ref, wpa_ref, wpb_ref, wo_ref, post_ref, out_ref, *, blocks_per_seq):
    tm = h_ref.shape[0]
    i = pl.program_id(0)

    z = z_ref[...].astype(F32)
    pos = i % blocks_per_seq
    zp = zp_ref[...].astype(F32)
    zn = zn_ref[...].astype(F32)
    prev_row = jnp.where(pos == 0, 0.0, zp[zp.shape[0] - 1:, :])
    next_row = jnp.where(pos == blocks_per_seq - 1, 0.0, zn[0:1, :])
    rows = lax.broadcasted_iota(jnp.int32, z.shape, 0)
    z_prev = jnp.where(rows == 0, prev_row, pltpu.roll(z, 1, axis=0))
    z_next = jnp.where(rows == tm - 1, next_row, pltpu.roll(z, tm - 1, axis=0))
    conv = z_prev * cw_ref[0:1, :] + z * cw_ref[1:2, :] + z_next * cw_ref[2:3, :]
    yb = (gb_ref[...].astype(F32) * conv).astype(BF16)

    half = tm // 2
    for r in range(2):
        sl = slice(r * half, (r + 1) * half)
        o = jnp.concatenate([o_ref[0, h, sl, :] for h in range(N_HEADS)], axis=1)
        y_att = _dot(o, wpa_ref[...])
        y_conv = _dot(yb[sl], wpb_ref[...])
        merged = sga_ref[sl, :].astype(F32) * y_att + sgb_ref[sl, :].astype(F32) * y_conv
        y = _dot(merged.astype(BF16), wo_ref[...])
        out_ref[sl, :] = h_ref[sl, :] + mod_ref[0, 5:6, :] * _rms(y, post_ref[...])


def _mixout(h, mods, o, z, gb, sga, sgb, conv_w, wpa, wpb, wo, post_g, *, seq):
    rows, d = h.shape
    tm = ROW_BLOCK
    bpb = seq // tm
    halo = 16
    hpb = tm // halo
    n_halo = rows // halo
    row_spec = pl.BlockSpec((tm, d), lambda i: (i, 0))
    return pl.pallas_call(
        functools.partial(_mixout_kernel, blocks_per_seq=bpb),
        grid=(rows // tm,),
        in_specs=[
            row_spec,
            pl.BlockSpec((1, N_MOD, d), lambda i: (i // bpb, 0, 0)),
            pl.BlockSpec((1, N_HEADS, tm, LANES), lambda i: (i // bpb, 0, i % bpb, 0)),
            row_spec,
            pl.BlockSpec((halo, d), lambda i: (jnp.maximum(i * hpb - 1, 0), 0)),
            pl.BlockSpec((halo, d), lambda i: (jnp.minimum((i + 1) * hpb, n_halo - 1), 0)),
            row_spec, row_spec, row_spec,
            _const_spec(conv_w.shape),
            _const_spec(wpa.shape), _const_spec(wpb.shape), _const_spec(wo.shape),
            _const_spec((1, d)),
        ],
        out_specs=row_spec,
        out_shape=jax.ShapeDtypeStruct((rows, d), F32),
        compiler_params=_params(1),
        name="mixout",
    )(h, mods, o, z, z, z, gb, sga, sgb, conv_w, wpa, wpb, wo, post_g)


def kernel(x, c, ctx, c_ctx, w_mod, b_mod, ffn1_pre_g, ffn1_post_g, ffn1_w_gate, ffn1_w_up, ffn1_w_down, mix_pre_g, mix_post_g, w_in, lam_q1, lam_k1, lam_q2, lam_k2, attn_subln_g, conv_w, w_attn_proj, w_conv_proj, w_out, ffn2_pre_g, ffn2_post_g, ffn2_w_gate, ffn2_w_up, ffn2_w_down):
    batch, seq, d = x.shape
    ctx_len = ctx.shape[1]
    assert w_mod.shape[0] == 1, "single-layer stack only"
    assert batch + 1 <= MOD_ROWS and seq % ROW_BLOCK == 0 and seq % GRID_W == 0

    cond = jnp.zeros((MOD_ROWS, d), F32).at[:batch].set(c).at[batch].set(c_ctx)
    mods = _adaln(cond, w_mod[0], b_mod[0]).reshape(MOD_ROWS, N_MOD, d)

    def row(v):
        return v.reshape(1, -1)

    ffn1_w = [w[0].astype(BF16) for w in (ffn1_w_gate, ffn1_w_up, ffn1_w_down)]

    h, w_in_b, wpa, wpb, wo, *ffn2_w = _ffn(
        x.reshape(batch * seq, d), mods, row(ffn1_pre_g), row(ffn1_post_g), *ffn1_w,
        mod_base=0, mod_row0=0, rows_per_mod_row=seq, tm=FFN_ROW_BLOCK,
        cast=(w_in[0], w_attn_proj[0], w_conv_proj[0], w_out[0],
              ffn2_w_gate[0], ffn2_w_up[0], ffn2_w_down[0]))
    hc, = _ffn(ctx.reshape(batch * ctx_len, d), mods, row(ffn1_pre_g), row(ffn1_post_g), *ffn1_w,
               mod_base=0, mod_row0=batch, rows_per_mod_row=batch * ctx_len, tm=batch * ctx_len)

    qt, k_all, vt_all, z, gb, sga, sgb = _proj(h, mods, row(mix_pre_g), w_in_b,
                                               _rope_tables(seq // GRID_W, ROW_BLOCK),
                                               batch=batch, seq=seq, n_keys=seq + ctx_len)
    k_all, vt_all = _ctx_kv(hc, mods, row(mix_pre_g), w_in_b, k_all, vt_all,
                            batch=batch, ctx_len=ctx_len, mod_row=batch)
    lam_vecs = jnp.concatenate([lam_q1, lam_k1, lam_q2, lam_k2], axis=0)
    o, = _attn(lam_vecs, attn_subln_g.reshape(V_DIM, 1), qt, k_all, vt_all)
    h = _mixout(h, mods, o, z, gb, sga, sgb, conv_w[0], wpa, wpb, wo, row(mix_post_g), seq=seq)
    h, = _ffn(h, mods, row(ffn2_pre_g), row(ffn2_post_g), *ffn2_w,
              mod_base=6, mod_row0=0, rows_per_mod_row=seq, tm=FFN_ROW_BLOCK)
    return h.reshape(batch, seq, d)
```
